```python
import jax, jax.numpy as jnp
from jax import lax
import numpy as np

D_MODEL = 1024
BATCH = 8
SEQ = 2048
DEPTH = 4
DEC_BATCH = 32
DEC_SEQ = 4
PAST_LEN = 8192
PAGE_SIZE = 128

N_MIXERS = 3
N_GLA = (DEPTH + 2) // 3
N_FOX = (DEPTH + 1) // 3
N_CONV = DEPTH // 3

GLA_HEADS = 4
GLA_DK = D_MODEL // 2 // GLA_HEADS
GLA_DV = D_MODEL // GLA_HEADS
GLA_RANK = 16
GLA_TAU = 16.0
GLA_CHUNK = 64

FOX_HEADS = 16
FOX_DH = D_MODEL // FOX_HEADS
FOX_SCALE = FOX_DH ** -0.5
FOX_BIAS_INIT = 6.0
Q_BLOCK = 128
NEG_INF = -1e30

CONV_WIDTH = 31
D_FF = ((8 * D_MODEL // 3 + 255) // 256) * 256
RMS_EPS = 1e-6
LN_EPS = 1e-5

kernel_name = 'hybrid_gla_fox_conformer_decode_step'


def rmsnorm(x, g):
    xf = x.astype(jnp.float32)
    y = xf * lax.rsqrt(jnp.mean(xf * xf, axis=-1, keepdims=True) + RMS_EPS)
    return (y * g.astype(jnp.float32)).astype(x.dtype)


def swiglu_ffn(h, w1, w3, w2):
    return (jax.nn.silu(h @ w1) * (h @ w3)) @ w2


def gla_mixer(x, s0, wq, wk, wv, wg1, wg2, bg, wr, gn, wo):
    b, l, _ = x.shape
    f32 = jnp.float32
    q = (x @ wq).astype(f32).reshape(b, l, GLA_HEADS, GLA_DK) * (GLA_DK ** -0.5)
    k = (x @ wk).astype(f32).reshape(b, l, GLA_HEADS, GLA_DK)
    v = (x @ wv).astype(f32).reshape(b, l, GLA_HEADS, GLA_DV)
    logg = jax.nn.log_sigmoid(((x @ wg1) @ wg2 + bg).astype(f32)).reshape(b, l, GLA_HEADS, GLA_DK) / GLA_TAU
    c = GLA_CHUNK if l % GLA_CHUNK == 0 else l
    n = l // c

    def to_chunks(t):
        return t.reshape(b, n, c, GLA_HEADS, t.shape[-1]).transpose(1, 0, 3, 2, 4)

    causal = jnp.tril(jnp.ones((c, c), dtype=bool))

    def chunk_step(s, inp):
        qc, kc, vc, gc = inp
        cum = jnp.cumsum(gc, axis=2)
        q_e = qc * jnp.exp(cum)
        k_e = kc * jnp.exp(-cum)
        att = jnp.where(causal, jnp.einsum('bhtk,bhsk->bhts', q_e, k_e), 0.0)
        o = jnp.einsum('bhtk,bhkv->bhtv', q_e, s) + jnp.einsum('bhts,bhsv->bhtv', att, vc)
        last = cum[:, :, -1:, :]
        s_new = s * jnp.exp(last)[:, :, 0, :, None] + jnp.einsum('bhsk,bhsv->bhkv', kc * jnp.exp(last - cum), vc)
        return s_new, o

    s_fin, o = lax.scan(chunk_step, s0.astype(f32), (to_chunks(q), to_chunks(k), to_chunks(v), to_chunks(logg)))
    o = o.transpose(1, 0, 3, 2, 4).reshape(b, l, GLA_HEADS, GLA_DV)
    o = o * lax.rsqrt(jnp.mean(o * o, axis=-1, keepdims=True) + RMS_EPS) * gn.astype(f32)
    gate = jax.nn.silu((x @ wr).astype(f32)).reshape(b, l, GLA_HEADS, GLA_DV)
    y = (o * gate).reshape(b, l, GLA_HEADS * GLA_DV).astype(x.dtype) @ wo
    return y, s_fin


def fox_project(x, wq, wk, wv, wf, bf):
    b, l, _ = x.shape
    q = (x @ wq).reshape(b, l, FOX_HEADS, FOX_DH)
    k = (x @ wk).reshape(b, l, FOX_HEADS, FOX_DH)
    v = (x @ wv).reshape(b, l, FOX_HEADS, FOX_DH)
    logf = jax.nn.log_sigmoid((x @ wf + bf).astype(jnp.float32))
    return q, k, v, logf


def fox_prompt(x, wq, wk, wv, wf, bf, wo):
    b, l, _ = x.shape
    q, k, v, logf = fox_project(x, wq, wk, wv, wf, bf)
    cum = jnp.cumsum(logf, axis=1).transpose(0, 2, 1)
    nq = l // Q_BLOCK
    q_blocks = q.reshape(b, nq, Q_BLOCK, FOX_HEADS, FOX_DH).transpose(1, 0, 2, 3, 4)
    c_blocks = cum.reshape(b, FOX_HEADS, nq, Q_BLOCK).transpose(2, 0, 1, 3)
    q_pos = jnp.arange(l).reshape(nq, Q_BLOCK)
    k_pos = jnp.arange(l)

    def block(args):
        qb, cb, pb = args
        logits = (jnp.einsum('bqhd,bkhd->bhqk', qb, k).astype(jnp.float32) * FOX_SCALE
                  + cb[..., :, None] - cum[:, :, None, :])
        logits = jnp.where(pb[:, None] >= k_pos[None, :], logits, NEG_INF)
        p = jax.nn.softmax(logits, axis=-1).astype(v.dtype)
        return jnp.einsum('bhqk,bkhd->bqhd', p, v)

    o = lax.map(block, (q_blocks, c_blocks, q_pos))
    o = o.transpose(1, 0, 2, 3, 4).reshape(b, l, FOX_HEADS * FOX_DH)
    return o @ wo, k, v, logf


def fox_sample(x, k_pool, v_pool, lf_pool, page_table, wq, wk, wv, wf, bf, wo):
    b, n, _ = x.shape
    q, k, v, logf = fox_project(x, wq, wk, wv, wf, bf)
    k_past = k_pool[page_table].reshape(b, -1, FOX_HEADS, FOX_DH)
    v_past = v_pool[page_table].reshape(b, -1, FOX_HEADS, FOX_DH)
    lf_past = lf_pool[page_table].reshape(b, -1, FOX_HEADS).astype(jnp.float32)
    p_len = k_past.shape[1]
    k_all = jnp.concatenate([k_past, k.astype(k_past.dtype)], axis=1)
    v_all = jnp.concatenate([v_past, v.astype(v_past.dtype)], axis=1)
    lf_all = jnp.concatenate([lf_past, logf], axis=1)
    suffix = (lax.cumsum(lf_all, axis=1, reverse=True) - lf_all).transpose(0, 2, 1)
    logits = (jnp.einsum('bqhd,bkhd->bhqk', q.astype(k_all.dtype), k_all).astype(jnp.float32) * FOX_SCALE
              + suffix[:, :, None, :] - suffix[:, :, p_len:, None])
    mask = jnp.arange(p_len + n)[None, :] <= (p_len + jnp.arange(n))[:, None]
    logits = jnp.where(mask, logits, NEG_INF)
    p = jax.nn.softmax(logits, axis=-1).astype(v_all.dtype)
    o = jnp.einsum('bhqk,bkhd->bqhd', p, v_all).reshape(b, n, FOX_HEADS * FOX_DH).astype(x.dtype)
    return o @ wo, k, v, logf


def conv_module(x, buf, w1, b1, wdw, bdw, lng, lnb, w2, b2):
    u = x @ w1 + b1
    a, g = jnp.split(u, 2, axis=-1)
    u = a * jax.nn.sigmoid(g)
    ext = jnp.concatenate([buf.astype(u.dtype), u], axis=1)
    y = lax.conv_general_dilated(ext, wdw[:, None, :].astype(u.dtype), window_strides=(1,), padding='VALID',
                                 dimension_numbers=('NWC', 'WIO', 'NWC'), feature_group_count=D_MODEL) + bdw
    yf = y.astype(jnp.float32)
    mu = jnp.mean(yf, axis=-1, keepdims=True)
    var = jnp.mean(jnp.square(yf - mu), axis=-1, keepdims=True)
    yf = (yf - mu) * lax.rsqrt(var + LN_EPS) * lng.astype(jnp.float32) + lnb.astype(jnp.float32)
    out = jax.nn.silu(yf).astype(x.dtype) @ w2 + b2
    return out, ext[:, -(CONV_WIDTH - 1):, :]


def setup_inputs(seed: int = 0) -> dict:
    key = jax.random.key(seed)
    ks = iter(jax.random.split(key, 48))
    f32 = jnp.float32
    n_pages = PAST_LEN // PAGE_SIZE
    n_pool = (5 * DEC_BATCH * n_pages + 3) // 4

    def nrm(shape, scale=1.0):
        return jax.random.normal(next(ks), shape, f32) * scale

    def w(shape, fan_in):
        return nrm(shape, fan_in ** -0.5)

    d = D_MODEL
    inp = {}
    inp['x_prompt'] = nrm((BATCH, SEQ, d))
    inp['x_sample'] = nrm((DEC_BATCH, DEC_SEQ, d))
    inp['state_gla'] = nrm((N_GLA, DEC_BATCH, GLA_HEADS, GLA_DK, GLA_DV), 0.5)
    inp['cache_fox_k'] = nrm((N_FOX, n_pool, PAGE_SIZE, FOX_HEADS, FOX_DH))
    inp['cache_fox_v'] = nrm((N_FOX, n_pool, PAGE_SIZE, FOX_HEADS, FOX_DH))
    inp['cache_fox_logf'] = jax.nn.log_sigmoid(FOX_BIAS_INIT + nrm((N_FOX, n_pool, PAGE_SIZE, FOX_HEADS), 0.5))
    inp['state_conv'] = nrm((N_CONV, DEC_BATCH, CONV_WIDTH - 1, d), 0.5)
    perm = jax.random.permutation(next(ks), n_pool)
    inp['page_table'] = perm[:DEC_BATCH * n_pages].reshape(DEC_BATCH, n_pages).astype(jnp.int32)
    inp['norm_g'] = 1.0 + nrm((DEPTH, 4, d), 0.05)
    inp['gla_wq'] = w((N_GLA, d, GLA_HEADS * GLA_DK), d)
    inp['gla_wk'] = w((N_GLA, d, GLA_HEADS * GLA_DK), d)
    inp['gla_wv'] = w((N_GLA, d, GLA_HEADS * GLA_DV), d)
    inp['gla_wg1'] = w((N_GLA, d, GLA_RANK), d)
    inp['gla_wg2'] = w((N_GLA, GLA_RANK, GLA_HEADS * GLA_DK), GLA_RANK)
    inp['gla_bg'] = nrm((N_GLA, GLA_HEADS * GLA_DK), 0.1)
    inp['gla_wr'] = w((N_GLA, d, GLA_HEADS * GLA_DV), d)
    inp['gla_gn'] = 1.0 + nrm((N_GLA, GLA_DV), 0.05)
    inp['gla_wo'] = w((N_GLA, GLA_HEADS * GLA_DV, d), GLA_HEADS * GLA_DV)
    inp['fox_wq'] = w((N_FOX, d, d), d)
    inp['fox_wk'] = w((N_FOX, d, d), d)
    inp['fox_wv'] = w((N_FOX, d, d), d)
    inp['fox_wf'] = nrm((N_FOX, d, FOX_HEADS), 0.2 * d ** -0.5)
    inp['fox_bf'] = FOX_BIAS_INIT + nrm((N_FOX, FOX_HEADS), 0.5)
    inp['fox_wo'] = w((N_FOX, d, d), d)
    inp['conv_w1'] = w((N_CONV, d, 2 * d), d)
    inp['conv_b1'] = nrm((N_CONV, 2 * d), 0.02)
    inp['conv_wdw'] = w((N_CONV, CONV_WIDTH, d), CONV_WIDTH)
    inp['conv_bdw'] = nrm((N_CONV, d), 0.02)
    inp['conv_ln_g'] = 1.0 + nrm((N_CONV, d), 0.05)
    inp['conv_ln_b'] = nrm((N_CONV, d), 0.02)
    inp['conv_w2'] = w((N_CONV, d, d), d)
    inp['conv_b2'] = nrm((N_CONV, d), 0.02)
    inp['ffn_w1'] = w((DEPTH, d, D_FF), d)
    inp['ffn_w3'] = w((DEPTH, d, D_FF), d)
    inp['ffn_w2'] = w((DEPTH, D_FF, d), D_FF)
    return inp


def reference(x_prompt, x_sample, state_gla, cache_fox_k, cache_fox_v, cache_fox_logf, state_conv, page_table,
              norm_g, gla_wq, gla_wk, gla_wv, gla_wg1, gla_wg2, gla_bg, gla_wr, gla_gn, gla_wo,
              fox_wq, fox_wk, fox_wv, fox_wf, fox_bf, fox_wo,
              conv_w1, conv_b1, conv_wdw, conv_bdw, conv_ln_g, conv_ln_b, conv_w2, conv_b2,
              ffn_w1, ffn_w3, ffn_w2):
    xp, xs = x_prompt, x_sample
    bp = xp.shape[0]
    gla_p, gla_s = [], []
    fk_p, fv_p, fl_p, fk_s, fv_s, fl_s = [], [], [], [], [], []
    cv_p, cv_s = [], []
    for i in range(DEPTH):
        j = i // N_MIXERS
        kind = i % N_MIXERS
        hp = rmsnorm(xp, norm_g[i, 0])
        hs = rmsnorm(xs, norm_g[i, 0])
        if kind == 0:
            gw = (gla_wq[j], gla_wk[j], gla_wv[j], gla_wg1[j], gla_wg2[j], gla_bg[j], gla_wr[j], gla_gn[j], gla_wo[j])
            s0 = jnp.zeros((bp, GLA_HEADS, GLA_DK, GLA_DV), jnp.float32)
            mp, sp = gla_mixer(hp, s0, *gw)
            ms, ss = gla_mixer(hs, state_gla[j], *gw)
            gla_p.append(sp)
            gla_s.append(ss)
        elif kind == 1:
            fw = (fox_wq[j], fox_wk[j], fox_wv[j], fox_wf[j], fox_bf[j], fox_wo[j])
            mp, kp, vp, lp = fox_prompt(hp, *fw)
            ms, kn, vn, ln = fox_sample(hs, cache_fox_k[j], cache_fox_v[j], cache_fox_logf[j], page_table, *fw)
            fk_p.append(kp)
            fv_p.append(vp)
            fl_p.append(lp)
            fk_s.append(kn)
            fv_s.append(vn)
            fl_s.append(ln)
        else:
            cw = (conv_w1[j], conv_b1[j], conv_wdw[j], conv_bdw[j], conv_ln_g[j], conv_ln_b[j], conv_w2[j], conv_b2[j])
            buf0 = jnp.zeros((bp, CONV_WIDTH - 1, D_MODEL), hp.dtype)
            mp, bufp = conv_module(hp, buf0, *cw)
            ms, bufs = conv_module(hs, state_conv[j], *cw)
            cv_p.append(bufp)
            cv_s.append(bufs)
        xp = xp + rmsnorm(mp, norm_g[i, 1])
        xs = xs + rmsnorm(ms, norm_g[i, 1])
        xp = xp + rmsnorm(swiglu_ffn(rmsnorm(xp, norm_g[i, 2]), ffn_w1[i], ffn_w3[i], ffn_w2[i]), norm_g[i, 3])
        xs = xs + rmsnorm(swiglu_ffn(rmsnorm(xs, norm_g[i, 2]), ffn_w1[i], ffn_w3[i], ffn_w2[i]), norm_g[i, 3])
    return (xp, xs, jnp.stack(gla_p), jnp.stack(gla_s),
            jnp.stack(fk_p), jnp.stack(fv_p), jnp.stack(fl_p),
            jnp.stack(fk_s), jnp.stack(fv_s), jnp.stack(fl_s),
            jnp.stack(cv_p), jnp.stack(cv_s))
```

```python
import functools

import numpy as np
import jax
import jax.numpy as jnp
from jax import lax
from jax.experimental import pallas as pl
from jax.experimental.pallas import tpu as pltpu

F32 = jnp.float32
BF16 = jnp.bfloat16

D_MODEL = 1024
DEPTH = 4
N_MIXERS = 3
GLA_HEADS = 4
GLA_DK = 128
GLA_DV = 256
GLA_TAU = 16.0
GLA_CHUNK = 64
FOX_HEADS = 16
FOX_DH = 64
FOX_SCALE = FOX_DH ** -0.5
PAGE_SIZE = 128
NEG_INF = -1e30
CONV_WIDTH = 31
CONV_HDR = 32
D_FF = 2816
RMS_EPS = 1e-6
LN_EPS = 1e-5

LANES = 128
VMEM_LIMIT = 52 * 1024 * 1024


def _dot(a, b):
    return jnp.dot(a, b, preferred_element_type=F32)


def _dot_nt(a, b):
    return lax.dot_general(a, b, (((1,), (1,)), ((), ())), preferred_element_type=F32)


def _dot_tn(a, b):
    return lax.dot_general(a, b, (((0,), (0,)), ((), ())), preferred_element_type=F32)


def _rms(x, g):
    return x * lax.rsqrt(jnp.mean(x * x, axis=-1, keepdims=True) + RMS_EPS) * g


def _silu(x):
    return x * jax.nn.sigmoid(x)


def _log_sigmoid(z):
    return jnp.minimum(z, 0.0) - jnp.log1p(jnp.exp(-jnp.abs(z)))


def _split3(a):
    hi = a.astype(BF16)
    r1 = a - hi.astype(F32)
    mid = r1.astype(BF16)
    lo = (r1 - mid.astype(F32)).astype(BF16)
    return hi, mid, lo


def _dot3_left(m, a):
    hi, mid, lo = _split3(a)
    return _dot(m, hi) + _dot(m, mid) + _dot(m, lo)


def _dot3_right(a, m):
    hi, mid, lo = _split3(a)
    return _dot(hi, m) + _dot(mid, m) + _dot(lo, m)


def _const_spec(shape):
    nd = len(shape)
    return pl.BlockSpec(shape, lambda *_: (0,) * nd, pipeline_mode=pl.Buffered(1))


def _params(sem):
    return pltpu.CompilerParams(dimension_semantics=sem, vmem_limit_bytes=VMEM_LIMIT)


_FF_CHUNKS = tuple((s, min(512, D_FF - s)) for s in range(0, D_FF, 512))


def _ffn_body(x_ref, g_ref, w1_ref, w3_ref, w2_ref, o_ref, acc_ref):
    x = x_ref[...]
    h = _rms(x, g_ref[0:1, :]).astype(BF16)
    for idx, (s, n) in enumerate(_FF_CHUNKS):
        a = _dot(h, w1_ref[:, s:s + n])
        b = _dot(h, w3_ref[:, s:s + n])
        u = (_silu(a) * b).astype(BF16)
        y = _dot(u, w2_ref[s:s + n, :])
        if idx == 0:
            acc_ref[...] = y
        else:
            acc_ref[...] += y
    o_ref[...] = x + _rms(acc_ref[...], g_ref[1:2, :])


def _ffn(x, g2, w1, w3, w2, tm):
    t, d = x.shape
    return pl.pallas_call(
        _ffn_body,
        grid=(t // tm,),
        in_specs=[pl.BlockSpec((tm, d), lambda i: (i, 0)),
                  _const_spec(g2.shape), _const_spec(w1.shape), _const_spec(w3.shape), _const_spec(w2.shape)],
        out_specs=pl.BlockSpec((tm, d), lambda i: (i, 0)),
        out_shape=jax.ShapeDtypeStruct((t, d), F32),
        scratch_shapes=[pltpu.VMEM((tm, d), F32)],
        compiler_params=_params(("arbitrary",)),
        name="ffn",
    )(x, g2, w1, w3, w2)


def _proj_res_body(x_ref, o_ref, w_ref, b_ref, g_ref, out_ref):
    y = _dot(o_ref[...].astype(BF16), w_ref[...]) + b_ref[...]
    out_ref[...] = x_ref[...] + _rms(y, g_ref[...])


def _proj_res(x, o, w, b, g, tm):
    t, d = x.shape
    return pl.pallas_call(
        _proj_res_body,
        grid=(t // tm,),
        in_specs=[pl.BlockSpec((tm, d), lambda i: (i, 0)),
                  pl.BlockSpec((tm, o.shape[1]), lambda i: (i, 0)),
                  _const_spec(w.shape), _const_spec(b.shape), _const_spec(g.shape)],
        out_specs=pl.BlockSpec((tm, d), lambda i: (i, 0)),
        out_shape=jax.ShapeDtypeStruct((t, d), F32),
        compiler_params=_params(("arbitrary",)),
        name="proj_res",
    )(x, o, w, b, g)


def _gla_body(*refs, tl, valid, has_s0, multi_batch):
    if has_s0:
        x_ref, s0_ref = refs[0], refs[1]
        rest = refs[2:]
    else:
        x_ref, s0_ref = refs[0], None
        rest = refs[1:]
    (g_ref, wq, wk, wv, wr, wg1, wg2, bg, gn, wo,
     o_ref, s_ref, q_s, k_s, lg_s, v_s, gt_s, og_s) = rest
    c = GLA_CHUNK
    rows = x_ref.shape[0]
    n_chunks = rows // c
    j = pl.program_id(1)

    x = x_ref[...]
    h = _rms(x, g_ref[0:1, :]).astype(BF16)
    q_s[...] = _dot(h, wq[...]) * (GLA_DK ** -0.5)
    k = _dot(h, wk[...])
    z = _dot(_dot(h, wg1[...]).astype(BF16), wg2[...]) + bg[...]
    lg = _log_sigmoid(z) * (1.0 / GLA_TAU)
    if valid < tl:
        assert tl & (tl - 1) == 0
        row = lax.broadcasted_iota(jnp.int32, (rows, 1), 0) & (tl - 1)
        keep = row < valid
        k = jnp.where(keep, k, 0.0)
        lg = jnp.where(keep, lg, 0.0)
    k_s[...] = k
    lg_s[...] = lg
    v_s[...] = _dot(h, wv[...]).astype(BF16)
    gt_s[...] = _silu(_dot(h, wr[...]))

    @pl.when(j == 0)
    def _():
        if has_s0:
            s_ref[...] = s0_ref[...]
        else:
            s_ref[...] = jnp.zeros(s_ref.shape, F32)

    ri = lax.broadcasted_iota(jnp.int32, (c, c), 0)
    ci = lax.broadcasted_iota(jnp.int32, (c, c), 1)
    causal = ri >= ci
    tri = jnp.where(causal, 1.0, 0.0).astype(BF16)
    eye = (lax.broadcasted_iota(jnp.int32, (GLA_DK, GLA_DK), 0)
           == lax.broadcasted_iota(jnp.int32, (GLA_DK, GLA_DK), 1))

    def chunk_body(i, carry):
        r0 = pl.multiple_of(i * c, c)
        nb = i if multi_batch else 0
        cum = _dot3_left(tri, lg_s[pl.ds(r0, c), :])
        for hd in range(GLA_HEADS):
            ks = slice(hd * GLA_DK, (hd + 1) * GLA_DK)
            vs = slice(hd * GLA_DV, (hd + 1) * GLA_DV)
            cum_h = cum[:, ks]
            k_h = k_s[pl.ds(r0, c), ks]
            qe = (q_s[pl.ds(r0, c), ks] * jnp.exp(cum_h)).astype(BF16)
            ke = (k_h * jnp.exp(-cum_h)).astype(BF16)
            att = jnp.where(causal, _dot_nt(qe, ke), 0.0).astype(BF16)
            vc = v_s[pl.ds(r0, c), vs]
            s_old = s_ref[nb, hd]
            o = _dot(qe, s_old.astype(BF16)) + _dot(att, vc)
            last = cum_h[c - 1:c, :]
            kd = (k_h * jnp.exp(last - cum_h)).astype(BF16)
            dcol = jnp.sum(jnp.where(eye, jnp.exp(last), 0.0), axis=1, keepdims=True)
            s_ref[nb, hd] = s_old * dcol + _dot_tn(kd, vc)
            on = o * lax.rsqrt(jnp.mean(o * o, axis=-1, keepdims=True) + RMS_EPS) * gn[...]
            og_s[pl.ds(r0, c), vs] = (on * gt_s[pl.ds(r0, c), vs]).astype(BF16)
        return carry

    lax.fori_loop(0, n_chunks, chunk_body, 0)
    y = _dot(og_s[...], wo[...])
    o_ref[...] = x + _rms(y, g_ref[1:2, :])


def _gla_layer(x, s0, g2, w, *, n_batch, seq, nb, tl, valid):
    t, d = x.shape
    rows = nb * tl
    nl = seq // tl
    assert nb == 1 or nl == 1
    hk, hv = GLA_HEADS * GLA_DK, GLA_HEADS * GLA_DV
    has_s0 = s0 is not None
    st_spec = pl.BlockSpec((nb, GLA_HEADS, GLA_DK, GLA_DV), lambda i, j: (i, 0, 0, 0))
    in_specs = [pl.BlockSpec((rows, d), lambda i, j: (i * nl + j, 0))]
    args = [x]
    if has_s0:
        in_specs.append(st_spec)
        args.append(s0)
    weights = [g2, w["wq"], w["wk"], w["wv"], w["wr"], w["wg1"], w["wg2"], w["bg"], w["gn"], w["wo"]]
    in_specs += [_const_spec(a.shape) for a in weights]
    args += weights
    body = functools.partial(_gla_body, tl=tl, valid=valid, has_s0=has_s0, multi_batch=nb > 1)
    return pl.pallas_call(
        body,
        grid=(n_batch // nb, nl),
        in_specs=in_specs,
        out_specs=[pl.BlockSpec((rows, d), lambda i, j: (i * nl + j, 0)), st_spec],
        out_shape=[jax.ShapeDtypeStruct((t, d), F32),
                   jax.ShapeDtypeStruct((n_batch, GLA_HEADS, GLA_DK, GLA_DV), F32)],
        scratch_shapes=[pltpu.VMEM((rows, hk), F32), pltpu.VMEM((rows, hk), F32), pltpu.VMEM((rows, hk), F32),
                        pltpu.VMEM((rows, hv), BF16), pltpu.VMEM((rows, hv), F32), pltpu.VMEM((rows, hv), BF16)],
        compiler_params=_params(("arbitrary", "arbitrary")),
        name="gla_layer",
    )(*args)


def _fox_bias_tables():
    pq = np.zeros((3 * LANES, D_MODEL), np.float32)
    pk = np.zeros((3 * LANES, D_MODEL), np.float32)
    oq = np.zeros((1, D_MODEL), np.float32)
    ok = np.zeros((1, D_MODEL), np.float32)
    for hh in range(FOX_HEADS):
        for i in range(3):
            pq[i * LANES + hh, hh * FOX_DH + i] = 1.0
            ok[0, hh * FOX_DH + i] = 1.0
            oq[0, hh * FOX_DH + 3 + i] = 1.0
            pk[i * LANES + hh, hh * FOX_DH + 3 + i] = -1.0
    return (jnp.asarray(pq, BF16), jnp.asarray(pk, BF16), jnp.asarray(oq, F32), jnp.asarray(ok, F32))


def _fox_proj_prompt_body(x_ref, g_ref, wq, wk, wv, wf, bf, pq, pk, oq, ok, tri_ref,
                          k_out, v_out, lf_out, q2_out, k2_out, vb_out, carry_ref):
    tl = x_ref.shape[0]

    @pl.when(pl.program_id(1) == 0)
    def _():
        carry_ref[...] = jnp.zeros(carry_ref.shape, F32)

    h = _rms(x_ref[...], g_ref[...]).astype(BF16)
    q = (_dot(h, wq[...]) * FOX_SCALE).astype(BF16)
    k = _dot(h, wk[...])
    v = _dot(h, wv[...])
    k_out[...] = k
    v_out[...] = v
    vb_out[...] = v.astype(BF16)
    lf = _log_sigmoid(_dot(h, wf[...]) + bf[...])
    lf_out[...] = lf[:, :FOX_HEADS]
    cum = _dot3_left(tri_ref[...], lf) + carry_ref[...]
    carry_ref[...] = cum[tl - 1:tl, :]
    c3 = jnp.concatenate(_split3(cum), axis=1)
    qb = (_dot(c3, pq[...]) + oq[...]).astype(BF16)
    kb = (_dot(c3, pk[...]) + ok[...]).astype(BF16)
    kh = k.astype(BF16)
    for hp in range(FOX_HEADS // 2):
        a, b = hp * LANES, (hp + 1) * LANES
        q2_out[:, 2 * a:2 * a + LANES] = q[:, a:b]
        q2_out[:, 2 * a + LANES:2 * b] = qb[:, a:b]
        k2_out[:, 2 * a:2 * a + LANES] = kh[:, a:b]
        k2_out[:, 2 * a + LANES:2 * b] = kb[:, a:b]


def _fox_proj_prompt(x, g, w, tabs, *, n_batch, seq, tl):
    t, d = x.shape
    nl = seq // tl
    pq, pk, oq, ok = tabs
    tri = jnp.asarray(np.tril(np.ones((tl, tl), np.float32)), BF16)
    consts = [g, w["wq"], w["wk"], w["wv"], w["wf"], w["bf"], pq, pk, oq, ok, tri]
    row = lambda i, j: (i * nl + j, 0)
    return pl.pallas_call(
        _fox_proj_prompt_body,
        grid=(n_batch, nl),
        in_specs=[pl.BlockSpec((tl, d), row)] + [_const_spec(a.shape) for a in consts],
        out_specs=[pl.BlockSpec((tl, d), row), pl.BlockSpec((tl, d), row),
                   pl.BlockSpec((tl, FOX_HEADS), row),
                   pl.BlockSpec((tl, 2 * d), row), pl.BlockSpec((tl, 2 * d), row),
                   pl.BlockSpec((tl, d), row)],
        out_shape=[jax.ShapeDtypeStruct((t, d), F32), jax.ShapeDtypeStruct((t, d), F32),
                   jax.ShapeDtypeStruct((t, FOX_HEADS), F32),
                   jax.ShapeDtypeStruct((t, 2 * d), BF16), jax.ShapeDtypeStruct((t, 2 * d), BF16),
                   jax.ShapeDtypeStruct((t, d), BF16)],
        scratch_shapes=[pltpu.VMEM((1, LANES), F32)],
        compiler_params=_params(("arbitrary", "arbitrary")),
        name="fox_proj_prompt",
    )(x, *consts)


def _fox_attn_body(q_ref, k_ref, v_ref, o_ref, *, tq):
    qi = pl.program_id(2)
    q2 = q_ref[...]
    lane2 = lax.broadcasted_iota(jnp.int32, (1, 2 * LANES), 1) & (LANES - 1)
    sel = [jnp.where(lane2 < FOX_DH, 1.0, 0.0).astype(BF16),
           jnp.where(lane2 >= FOX_DH, 1.0, 0.0).astype(BF16)]
    ri = lax.broadcasted_iota(jnp.int32, (tq, tq), 0)
    ci = lax.broadcasted_iota(jnp.int32, (tq, tq), 1)

    def block(kj, carry, diag):
        k0 = pl.multiple_of(kj * tq, tq)
        kb = k_ref[pl.ds(k0, tq), :]
        vb = v_ref[pl.ds(k0, tq), :]
        out = []
        for hh in range(2):
            m, l, acc = carry[hh]
            s = _dot_nt(q2, kb * sel[hh])
            if diag:
                s = jnp.where(ri >= ci, s, NEG_INF)
            m_new = jnp.maximum(m, jnp.max(s, axis=1, keepdims=True))
            alpha = jnp.exp(m - m_new)
            p = jnp.exp(s - m_new)
            l = alpha * l + jnp.sum(p, axis=1, keepdims=True)
            acc = alpha * acc + _dot(p.astype(BF16), vb)
            out.append((m_new, l, acc))
        return tuple(out)

    init = tuple((jnp.full((tq, 1), NEG_INF, F32), jnp.zeros((tq, 1), F32), jnp.zeros((tq, LANES), F32))
                 for _ in range(2))
    carry = lax.fori_loop(0, qi, lambda kj, cr: block(kj, cr, False), init)
    (_, l0, a0), (_, l1, a1) = block(qi, carry, True)
    lane = lax.broadcasted_iota(jnp.int32, (1, LANES), 1)
    o_ref[...] = jnp.where(lane < FOX_DH, a0 / l0, a1 / l1).astype(BF16)


def _fox_attn(q2, k2, vb, *, n_batch, seq, tq):
    t = q2.shape[0]
    nq = seq // tq
    npair = FOX_HEADS // 2
    return pl.pallas_call(
        functools.partial(_fox_attn_body, tq=tq),
        grid=(n_batch, npair, nq),
        in_specs=[pl.BlockSpec((tq, 2 * LANES), lambda b, p, i: (b * nq + i, p)),
                  pl.BlockSpec((seq, 2 * LANES), lambda b, p, i: (b, p)),
                  pl.BlockSpec((seq, LANES), lambda b, p, i: (b, p))],
        out_specs=pl.BlockSpec((tq, LANES), lambda b, p, i: (b * nq + i, p)),
        out_shape=jax.ShapeDtypeStruct((t, D_MODEL), BF16),
        compiler_params=_params(("arbitrary", "arbitrary", "arbitrary")),
        name="fox_attn",
    )(q2, k2, vb)


def _fox_proj_sample_body(x_ref, g_ref, wq, wk, wv, wf, bf, q_out, k_out, v_out, lf_out):
    h = _rms(x_ref[...], g_ref[...]).astype(BF16)
    q_out[...] = _dot(h, wq[...]) * FOX_SCALE
    k_out[...] = _dot(h, wk[...])
    v_out[...] = _dot(h, wv[...])
    lf_out[...] = _log_sigmoid(_dot(h, wf[...]) + bf[...])[:, :FOX_HEADS]


def _fox_proj_sample(x, g, w):
    t, d = x.shape
    consts = [g, w["wq"], w["wk"], w["wv"], w["wf"], w["bf"]]
    full = lambda n: pl.BlockSpec((t, n), lambda i: (0, 0))
    return pl.pallas_call(
        _fox_proj_sample_body,
        grid=(1,),
        in_specs=[full(d)] + [_const_spec(a.shape) for a in consts],
        out_specs=[full(d), full(d), full(d), full(FOX_HEADS)],
        out_shape=[jax.ShapeDtypeStruct((t, d), F32)] * 3 + [jax.ShapeDtypeStruct((t, FOX_HEADS), F32)],
        compiler_params=_params(("arbitrary",)),
        name="fox_proj_sample",
    )(x, *consts)


def _page_suffix_body(lf_ref, sfx_ref, tot_ref):
    ji = lax.broadcasted_iota(jnp.int32, (PAGE_SIZE, PAGE_SIZE), 0)
    si = lax.broadcasted_iota(jnp.int32, (PAGE_SIZE, PAGE_SIZE), 1)
    later = jnp.where(ji > si, 1.0, 0.0).astype(BF16)
    ones = jnp.ones((PAGE_SIZE, PAGE_SIZE), BF16)
    hi, mid, lo = _split3(lf_ref[...])
    sfx_ref[...] = _dot(hi, later) + _dot(mid, later) + _dot(lo, later)
    tot_ref[...] = _dot(hi, ones) + _dot(mid, ones) + _dot(lo, ones)


def _page_suffix(lf_t, rows_per_step):
    r, p = lf_t.shape
    spec = pl.BlockSpec((rows_per_step, p), lambda i: (i, 0))
    return pl.pallas_call(
        _page_suffix_body,
        grid=(r // rows_per_step,),
        in_specs=[spec],
        out_specs=[spec, spec],
        out_shape=[jax.ShapeDtypeStruct((r, p), F32)] * 2,
        compiler_params=_params(("arbitrary",)),
        name="page_suffix",
    )(lf_t)


def _fox_decode_body(pt_ref, qbd_ref, kn_ref, vn_ref, lfn_ref, *refs, n_q, pages_per_step):
    pp = pages_per_step
    page_refs = refs[:4 * pp]
    o_ref, m_s, l_s, acc_s, run_s, ccol_s = refs[4 * pp:]
    del pt_ref
    j = pl.program_id(1)
    hq = FOX_HEADS * n_q
    qbd = qbd_ref[0]

    def attend(kb, vb, bias16, mask):
        bias = jnp.concatenate([bias16] * n_q, axis=0) + ccol_s[...]
        s = _dot_nt(qbd, kb) + bias
        if mask is not None:
            s = jnp.where(mask, s, NEG_INF)
        m_old = m_s[...]
        m_new = jnp.maximum(m_old, jnp.max(s, axis=1, keepdims=True))
        alpha = jnp.exp(m_old - m_new)
        p = jnp.exp(s - m_new)
        l_s[...] = alpha * l_s[...] + jnp.sum(p, axis=1, keepdims=True)
        acc_s[...] = alpha * acc_s[...] + _dot(p.astype(BF16), vb)
        m_s[...] = m_new

    @pl.when(j == 0)
    def _():
        lfn = lfn_ref[0]
        cn = lfn
        for sh in range(1, n_q):
            cn = cn + pltpu.roll(lfn, sh, axis=1)
        ccol_s[...] = jnp.concatenate(
            [jnp.broadcast_to(cn[:, qq:qq + 1], (FOX_HEADS, LANES)) for qq in range(n_q)], axis=0)
        m_s[...] = jnp.full(m_s.shape, NEG_INF, F32)
        l_s[...] = jnp.zeros(l_s.shape, F32)
        acc_s[...] = jnp.zeros(acc_s.shape, F32)
        run_s[...] = jnp.zeros(run_s.shape, F32)
        qrow = lax.broadcasted_iota(jnp.int32, (hq, LANES), 0) >> 4
        key = lax.broadcasted_iota(jnp.int32, (hq, LANES), 1)
        attend(kn_ref[0], vn_ref[0], -cn, key <= qrow)

    for i in range(pp):
        k_ref, v_ref, sfx_ref, tot_ref = page_refs[4 * i:4 * i + 4]
        attend(k_ref[0].astype(BF16), v_ref[0].astype(BF16), sfx_ref[0] + run_s[...], None)
        run_s[...] += tot_ref[0]

    @pl.when(j == pl.num_programs(1) - 1)
    def _():
        res = acc_s[...] / l_s[...]
        hrow = lax.broadcasted_iota(jnp.int32, (FOX_HEADS, D_MODEL), 0)
        hcol = lax.broadcasted_iota(jnp.int32, (FOX_HEADS, D_MODEL), 1) >> 6
        own = hrow == hcol
        rows = [jnp.sum(jnp.where(own, res[qq * FOX_HEADS:(qq + 1) * FOX_HEADS, :], 0.0), axis=0, keepdims=True)
                for qq in range(n_q)]
        o_ref[0] = jnp.concatenate(rows, axis=0)


def _fox_decode(page_table, qbd, k_new, v_new, lf_new_t, k_pool, v_pool, sfx, tot, *, n_q, pages_per_step):
    n_batch, n_pages = page_table.shape
    pp = pages_per_step
    hq = FOX_HEADS * n_q
    d = D_MODEL
    steps = n_pages // pp
    pt = page_table.reshape(-1)

    def page_map(i):
        def f(b, j, pt_ref):
            return (pt_ref[b * n_pages + (n_pages - 1 - (j * pp + i))], 0, 0)
        return f

    per_batch = lambda shape: pl.BlockSpec((1,) + shape, lambda b, j, pt_ref: (b, 0, 0))
    in_specs = [per_batch((hq, d)), per_batch((PAGE_SIZE, d)), per_batch((PAGE_SIZE, d)),
                per_batch((FOX_HEADS, LANES))]
    args = [qbd, k_new, v_new, lf_new_t]
    for i in range(pp):
        in_specs += [pl.BlockSpec((1, PAGE_SIZE, d), page_map(i)), pl.BlockSpec((1, PAGE_SIZE, d), page_map(i)),
                     pl.BlockSpec((1, FOX_HEADS, PAGE_SIZE), page_map(i)),
                     pl.BlockSpec((1, FOX_HEADS, PAGE_SIZE), page_map(i))]
        args += [k_pool, v_pool, sfx, tot]
    grid_spec = pltpu.PrefetchScalarGridSpec(
        num_scalar_prefetch=1,
        grid=(n_batch, steps),
        in_specs=in_specs,
        out_specs=pl.BlockSpec((1, n_q, d), lambda b, j, pt_ref: (b, 0, 0)),
        scratch_shapes=[pltpu.VMEM((hq, 1), F32), pltpu.VMEM((hq, 1), F32), pltpu.VMEM((hq, d), F32),
                        pltpu.VMEM((FOX_HEADS, LANES), F32), pltpu.VMEM((hq, LANES), F32)],
    )
    return pl.pallas_call(
        functools.partial(_fox_decode_body, n_q=n_q, pages_per_step=pp),
        grid_spec=grid_spec,
        out_shape=jax.ShapeDtypeStruct((n_batch, n_q, d), F32),
        compiler_params=_params(("arbitrary", "arbitrary")),
        name="fox_decode",
    )(pt, *args)


def _conv_tail(y, x, g_post, lng, lnb, w2, b2):
    mu = jnp.mean(y, axis=-1, keepdims=True)
    yc = y - mu
    var = jnp.mean(yc * yc, axis=-1, keepdims=True)
    yn = yc * lax.rsqrt(var + LN_EPS) * lng + lnb
    m = _dot(_silu(yn).astype(BF16), w2) + b2
    return x + _rms(m, g_post)


def _glu_in(x, g_pre, w1, b1):
    h = _rms(x, g_pre).astype(BF16)
    u2 = _dot(h, w1) + b1
    return u2[:, :D_MODEL] * jax.nn.sigmoid(u2[:, D_MODEL:])


_CONV_RB = 128


def _conv_prompt_body(x_ref, g_ref, w1, b1, wdw, bdw, lng, lnb, w2, b2, o_ref, st_ref, ext_ref, y_ref):
    tl = x_ref.shape[0]
    l = pl.program_id(1)
    x = x_ref[...]

    @pl.when(l == 0)
    def _():
        ext_ref[0:CONV_HDR, :] = jnp.zeros((CONV_HDR, D_MODEL), F32)

    ext_ref[CONV_HDR:CONV_HDR + tl, :] = _glu_in(x, g_ref[0:1, :], w1[...], b1[...])
    base = CONV_HDR - (CONV_WIDTH - 1)
    for r0 in range(0, tl, _CONV_RB):
        for c0 in range(0, D_MODEL, LANES):
            acc = jnp.broadcast_to(bdw[:, c0:c0 + LANES], (_CONV_RB, LANES))
            for wi in range(CONV_WIDTH):
                acc = acc + ext_ref[base + r0 + wi:base + r0 + wi + _CONV_RB, c0:c0 + LANES] * wdw[wi:wi + 1, c0:c0 + LANES]
            y_ref[r0:r0 + _CONV_RB, c0:c0 + LANES] = acc
    o_ref[...] = _conv_tail(y_ref[...], x, g_ref[1:2, :], lng[...], lnb[...], w2[...], b2[...])

    @pl.when(l == pl.num_programs(1) - 1)
    def _():
        st_ref[0] = ext_ref[CONV_HDR + tl - (CONV_WIDTH - 1):CONV_HDR + tl, :]

    ext_ref[0:CONV_HDR, :] = ext_ref[tl:tl + CONV_HDR, :]


def _conv_prompt(x, g2, w, *, n_batch, seq, tl):
    t, d = x.shape
    nl = seq // tl
    consts = [g2, w["w1"], w["b1"], w["wdw"], w["bdw"], w["lng"], w["lnb"], w["w2"], w["b2"]]
    row = lambda i, j: (i * nl + j, 0)
    return pl.pallas_call(
        _conv_prompt_body,
        grid=(n_batch, nl),
        in_specs=[pl.BlockSpec((tl, d), row)] + [_const_spec(a.shape) for a in consts],
        out_specs=[pl.BlockSpec((tl, d), row),
                   pl.BlockSpec((1, CONV_WIDTH - 1, d), lambda i, j: (i, 0, 0))],
        out_shape=[jax.ShapeDtypeStruct((t, d), F32),
                   jax.ShapeDtypeStruct((n_batch, CONV_WIDTH - 1, d), F32)],
        scratch_shapes=[pltpu.VMEM((CONV_HDR + tl, d), F32), pltpu.VMEM((tl, d), F32)],
        compiler_params=_params(("arbitrary", "arbitrary")),
        name="conv_prompt",
    )(x, *consts)


_SAMPLE_PAD = 8


def _conv_sample_body(x_ref, st_in, g_ref, w1, b1, wdw, bdw, lng, lnb, w2, b2, o_ref, st_out, ext_ref, u_ref, y_ref,
                      *, n_q):
    n_batch = st_in.shape[0]
    x = x_ref[...]
    u_ref[...] = _glu_in(x, g_ref[0:1, :], w1[...], b1[...])
    base = CONV_HDR - (CONV_WIDTH - 1)

    def per_batch(b, carry):
        r0 = pl.multiple_of(b * _SAMPLE_PAD, _SAMPLE_PAD)
        ext_ref[0:CONV_HDR, :] = st_in[b]
        ext_ref[CONV_HDR:CONV_HDR + _SAMPLE_PAD, :] = u_ref[pl.ds(r0, _SAMPLE_PAD), :]
        acc = jnp.broadcast_to(bdw[...], (_SAMPLE_PAD, D_MODEL))
        for wi in range(CONV_WIDTH):
            acc = acc + ext_ref[base + wi:base + wi + _SAMPLE_PAD, :] * wdw[wi:wi + 1, :]
        y_ref[pl.ds(r0, _SAMPLE_PAD), :] = acc
        st_out[b] = ext_ref[base + n_q:base + n_q + CONV_WIDTH - 1, :]
        return carry

    lax.fori_loop(0, n_batch, per_batch, 0)
    o_ref[...] = _conv_tail(y_ref[...], x, g_ref[1:2, :], lng[...], lnb[...], w2[...], b2[...])


def _conv_sample(x_pad, st_pad, g2, w, *, n_q):
    t, d = x_pad.shape
    n_batch = st_pad.shape[0]
    consts = [g2, w["w1"], w["b1"], w["wdw"], w["bdw"], w["lng"], w["lnb"], w["w2"], w["b2"]]
    return pl.pallas_call(
        functools.partial(_conv_sample_body, n_q=n_q),
        grid=(1,),
        in_specs=[pl.BlockSpec((t, d), lambda i: (0, 0)),
                  pl.BlockSpec(st_pad.shape, lambda i: (0, 0, 0))] + [_const_spec(a.shape) for a in consts],
        out_specs=[pl.BlockSpec((t, d), lambda i: (0, 0)),
                   pl.BlockSpec((n_batch, CONV_WIDTH - 1, d), lambda i: (0, 0, 0))],
        out_shape=[jax.ShapeDtypeStruct((t, d), F32),
                   jax.ShapeDtypeStruct((n_batch, CONV_WIDTH - 1, d), F32)],
        scratch_shapes=[pltpu.VMEM((CONV_HDR + _SAMPLE_PAD, d), F32), pltpu.VMEM((t, d), F32),
                        pltpu.VMEM((t, d), F32)],
        compiler_params=_params(("arbitrary",)),
        name="conv_sample",
    )(x_pad, st_pad, *consts)


def _row(v):
    return v.reshape(1, -1).astype(F32)


def _pad_cols(a, n):
    return jnp.pad(a, ((0, 0), (0, n - a.shape[1])))


def _gla_weights(wq, wk, wv, wg1, wg2, bg, wr, gn, wo):
    rank = wg1.shape[1]
    return dict(wq=wq.astype(BF16), wk=wk.astype(BF16), wv=wv.astype(BF16), wr=wr.astype(BF16),
                wg1=_pad_cols(wg1, LANES).astype(BF16),
                wg2=jnp.pad(wg2, ((0, LANES - rank), (0, 0))).astype(BF16),
                bg=_row(bg), gn=_row(gn), wo=wo.astype(BF16))


def _gla_sample(xs, s0, g2, w, *, n_batch, n_q):
    d = xs.shape[1]
    c = GLA_CHUNK
    xpad = jnp.pad(xs.reshape(n_batch, n_q, d), ((0, 0), (0, c - n_q), (0, 0))).reshape(n_batch * c, d)
    out, s_fin = _gla_layer(xpad, s0, g2, w, n_batch=n_batch, seq=c, nb=8, tl=c, valid=n_q)
    return out.reshape(n_batch, c, d)[:, :n_q].reshape(n_batch * n_q, d), s_fin


def _fox_sample(xs, g_pre, w, k_pool, v_pool, lf_pool, page_table, *, n_batch, n_q):
    d = D_MODEL
    q, k_new, v_new, lf_new = _fox_proj_sample(xs, g_pre, w)
    q4 = q.reshape(n_batch, n_q, 1, FOX_HEADS, FOX_DH)
    eye = jnp.eye(FOX_HEADS, dtype=F32)[None, None, :, :, None]
    qbd = (q4 * eye).reshape(n_batch, n_q * FOX_HEADS, d).astype(BF16)
    pad_keys = lambda a: jnp.pad(a.reshape(n_batch, n_q, d), ((0, 0), (0, PAGE_SIZE - n_q), (0, 0))).astype(BF16)
    lf_new_t = jnp.pad(lf_new.reshape(n_batch, n_q, FOX_HEADS).transpose(0, 2, 1),
                       ((0, 0), (0, 0), (0, LANES - n_q)))
    n_pool = lf_pool.shape[0]
    lf_t = lf_pool.transpose(0, 2, 1).reshape(n_pool * FOX_HEADS, PAGE_SIZE)
    sfx, tot = _page_suffix(lf_t, rows_per_step=n_pool * FOX_HEADS // 8)
    o = _fox_decode(page_table, qbd, pad_keys(k_new), pad_keys(v_new), lf_new_t,
                    k_pool.reshape(n_pool, PAGE_SIZE, d), v_pool.reshape(n_pool, PAGE_SIZE, d),
                    sfx.reshape(n_pool, FOX_HEADS, PAGE_SIZE), tot.reshape(n_pool, FOX_HEADS, PAGE_SIZE),
                    n_q=n_q, pages_per_step=4)
    return o.reshape(n_batch * n_q, d), k_new, v_new, lf_new


def kernel(x_prompt, x_sample, state_gla, cache_fox_k, cache_fox_v, cache_fox_logf, state_conv, page_table,
           norm_g, gla_wq, gla_wk, gla_wv, gla_wg1, gla_wg2, gla_bg, gla_wr, gla_gn, gla_wo,
           fox_wq, fox_wk, fox_wv, fox_wf, fox_bf, fox_wo,
           conv_w1, conv_b1, conv_wdw, conv_bdw, conv_ln_g, conv_ln_b, conv_w2, conv_b2,
           ffn_w1, ffn_w3, ffn_w2):
    bp, seq, d = x_prompt.shape
    bs, n_q, _ = x_sample.shape
    xp = x_prompt.reshape(bp * seq, d)
    xs = x_sample.reshape(bs * n_q, d)
    tp = 512
    gla_p, gla_s, cv_p, cv_s = [], [], [], []
    fox_out = None
    for i in range(DEPTH):
        j = i // N_MIXERS
        kind = i % N_MIXERS
        g_mix = norm_g[i, 0:2].astype(F32)
        if kind == 0:
            w = _gla_weights(gla_wq[j], gla_wk[j], gla_wv[j], gla_wg1[j], gla_wg2[j], gla_bg[j], gla_wr[j],
                             gla_gn[j], gla_wo[j])
            xp, sp = _gla_layer(xp, None, g_mix, w, n_batch=bp, seq=seq, nb=1, tl=tp, valid=tp)
            xs, ss = _gla_sample(xs, state_gla[j], g_mix, w, n_batch=bs, n_q=n_q)
            gla_p.append(sp)
            gla_s.append(ss)
        elif kind == 1:
            w = dict(wq=fox_wq[j].astype(BF16), wk=fox_wk[j].astype(BF16), wv=fox_wv[j].astype(BF16),
                     wf=_pad_cols(fox_wf[j], LANES).astype(BF16), bf=_pad_cols(_row(fox_bf[j]), LANES))
            wo = fox_wo[j].astype(BF16)
            zero_b = jnp.zeros((1, d), F32)
            g_pre, g_post = g_mix[0:1], g_mix[1:2]
            kp, vp, lfp, q2, k2, vb = _fox_proj_prompt(xp, g_pre, w, _fox_bias_tables(), n_batch=bp, seq=seq, tl=tp)
            op = _fox_attn(q2, k2, vb, n_batch=bp, seq=seq, tq=tp)
            xp = _proj_res(xp, op, wo, zero_b, g_post, tp)
            os_, kn, vn, lfn = _fox_sample(xs, g_pre, w, cache_fox_k[j], cache_fox_v[j], cache_fox_logf[j],
                                           page_table, n_batch=bs, n_q=n_q)
            xs = _proj_res(xs, os_, wo, zero_b, g_post, bs * n_q)
            fox_out = (kp.reshape(1, bp, seq, FOX_HEADS, FOX_DH), vp.reshape(1, bp, seq, FOX_HEADS, FOX_DH),
                       lfp.reshape(1, bp, seq, FOX_HEADS),
                       kn.reshape(1, bs, n_q, FOX_HEADS, FOX_DH), vn.reshape(1, bs, n_q, FOX_HEADS, FOX_DH),
                       lfn.reshape(1, bs, n_q, FOX_HEADS))
        else:
            w = dict(w1=conv_w1[j].astype(BF16), b1=_row(conv_b1[j]), wdw=conv_wdw[j].astype(F32),
                     bdw=_row(conv_bdw[j]), lng=_row(conv_ln_g[j]), lnb=_row(conv_ln_b[j]),
                     w2=conv_w2[j].astype(BF16), b2=_row(conv_b2[j]))
            xp, stp = _conv_prompt(xp, g_mix, w, n_batch=bp, seq=seq, tl=tp)
            xs_pad = jnp.pad(xs.reshape(bs, n_q, d), ((0, 0), (0, _SAMPLE_PAD - n_q), (0, 0)))
            st_pad = jnp.pad(state_conv[j], ((0, 0), (CONV_HDR - (CONV_WIDTH - 1), 0), (0, 0)))
            xs_pad, sts = _conv_sample(xs_pad.reshape(bs * _SAMPLE_PAD, d), st_pad, g_mix, w, n_q=n_q)
            xs = xs_pad.reshape(bs, _SAMPLE_PAD, d)[:, :n_q].reshape(bs * n_q, d)
            cv_p.append(stp)
            cv_s.append(sts)
        g_ffn = norm_g[i, 2:4].astype(F32)
        w1, w3, w2 = ffn_w1[i].astype(BF16), ffn_w3[i].astype(BF16), ffn_w2[i].astype(BF16)
        xp = _ffn(xp, g_ffn, w1, w3, w2, tp)
        xs = _ffn(xs, g_ffn, w1, w3, w2, bs * n_q)
    return (xp.reshape(bp, seq, d), xs.reshape(bs, n_q, d), jnp.stack(gla_p), jnp.stack(gla_s),
            *fox_out, jnp.stack(cv_p), jnp.stack(cv_s))
```

```python
import functools

import numpy as np
import jax
import jax.numpy as jnp
from jax import lax
from jax.experimental import pallas as pl
from jax.experimental.pallas import tpu as pltpu

F32 = jnp.float32
BF16 = jnp.bfloat16

D_MODEL = 1024
DEPTH = 4
N_MIXERS = 3
GLA_HEADS = 4
GLA_DK = 128
GLA_DV = 256
GLA_TAU = 16.0
GLA_CHUNK = 64
FOX_HEADS = 16
FOX_DH = 64
FOX_SCALE = FOX_DH ** -0.5
PAGE_SIZE = 128
NEG_INF = -1e30
CONV_WIDTH = 31
CONV_HDR = 32
D_FF = 2816
RMS_EPS = 1e-6
LN_EPS = 1e-5

LANES = 128
VMEM_LIMIT = 52 * 1024 * 1024


def _dot(a, b):
    return jnp.dot(a, b, preferred_element_type=F32)


def _dot_nt(a, b):
    return lax.dot_general(a, b, (((1,), (1,)), ((), ())), preferred_element_type=F32)


def _dot_tn(a, b):
    return lax.dot_general(a, b, (((0,), (0,)), ((), ())), preferred_element_type=F32)


def _rms(x, g):
    return x * lax.rsqrt(jnp.mean(x * x, axis=-1, keepdims=True) + RMS_EPS) * g


def _silu(x):
    return x * jax.nn.sigmoid(x)


def _log_sigmoid(z):
    return jnp.minimum(z, 0.0) - jnp.log1p(jnp.exp(-jnp.abs(z)))


def _split3(a):
    hi = a.astype(BF16)
    r1 = a - hi.astype(F32)
    mid = r1.astype(BF16)
    lo = (r1 - mid.astype(F32)).astype(BF16)
    return hi, mid, lo


def _dot3_left(m, a):
    hi, mid, lo = _split3(a)
    return _dot(m, hi) + _dot(m, mid) + _dot(m, lo)


def _dot3_right(a, m):
    hi, mid, lo = _split3(a)
    return _dot(hi, m) + _dot(mid, m) + _dot(lo, m)


def _const_spec(shape):
    nd = len(shape)
    return pl.BlockSpec(shape, lambda *_: (0,) * nd, pipeline_mode=pl.Buffered(1))


def _params(sem):
    return pltpu.CompilerParams(dimension_semantics=sem, vmem_limit_bytes=VMEM_LIMIT)


_FF_CHUNKS = tuple((s, min(512, D_FF - s)) for s in range(0, D_FF, 512))


def _ffn_body(x_ref, g_ref, w1_ref, w3_ref, w2_ref, o_ref, acc_ref):
    x = x_ref[...]
    h = _rms(x, g_ref[0:1, :]).astype(BF16)
    for idx, (s, n) in enumerate(_FF_CHUNKS):
        a = _dot(h, w1_ref[:, s:s + n])
        b = _dot(h, w3_ref[:, s:s + n])
        u = (_silu(a) * b).astype(BF16)
        y = _dot(u, w2_ref[s:s + n, :])
        if idx == 0:
            acc_ref[...] = y
        else:
            acc_ref[...] += y
    o_ref[...] = x + _rms(acc_ref[...], g_ref[1:2, :])


def _ffn(x, g2, w1, w3, w2, tm):
    t, d = x.shape
    return pl.pallas_call(
        _ffn_body,
        grid=(t // tm,),
        in_specs=[pl.BlockSpec((tm, d), lambda i: (i, 0)),
                  _const_spec(g2.shape), _const_spec(w1.shape), _const_spec(w3.shape), _const_spec(w2.shape)],
        out_specs=pl.BlockSpec((tm, d), lambda i: (i, 0)),
        out_shape=jax.ShapeDtypeStruct((t, d), F32),
        scratch_shapes=[pltpu.VMEM((tm, d), F32)],
        compiler_params=_params(("arbitrary",)),
        name="ffn",
    )(x, g2, w1, w3, w2)


def _proj_res_body(x_ref, o_ref, w_ref, b_ref, g_ref, out_ref):
    y = _dot(o_ref[...].astype(BF16), w_ref[...]) + b_ref[...]
    out_ref[...] = x_ref[...] + _rms(y, g_ref[...])


def _proj_res(x, o, w, b, g, tm):
    t, d = x.shape
    return pl.pallas_call(
        _proj_res_body,
        grid=(t // tm,),
        in_specs=[pl.BlockSpec((tm, d), lambda i: (i, 0)),
                  pl.BlockSpec((tm, o.shape[1]), lambda i: (i, 0)),
                  _const_spec(w.shape), _const_spec(b.shape), _const_spec(g.shape)],
        out_specs=pl.BlockSpec((tm, d), lambda i: (i, 0)),
        out_shape=jax.ShapeDtypeStruct((t, d), F32),
        compiler_params=_params(("arbitrary",)),
        name="proj_res",
    )(x, o, w, b, g)


def _gla_body(*refs, tl, valid, has_s0, multi_batch):
    if has_s0:
        x_ref, s0_ref = refs[0], refs[1]
        rest = refs[2:]
    else:
        x_ref, s0_ref = refs[0], None
        rest = refs[1:]
    (g_ref, wq, wk, wv, wr, wg1, wg2, bg, gn, wo,
     o_ref, s_ref, q_s, k_s, lg_s, v_s, gt_s, og_s, cum_s, qe_s, ke_s, kd_s, oi_s, u_s) = rest
    c = GLA_CHUNK
    rows = x_ref.shape[0]
    n_chunks = rows // c
    j = pl.program_id(1)

    x = x_ref[...]
    h = _rms(x, g_ref[0:1, :]).astype(BF16)
    q_s[...] = _dot(h, wq[...]) * (GLA_DK ** -0.5)
    k = _dot(h, wk[...])
    z = _dot(_dot(h, wg1[...]).astype(BF16), wg2[...]) + bg[...]
    lg = _log_sigmoid(z) * (1.0 / GLA_TAU)
    if valid < tl:
        assert tl & (tl - 1) == 0
        row = lax.broadcasted_iota(jnp.int32, (rows, 1), 0) & (tl - 1)
        keep = row < valid
        k = jnp.where(keep, k, 0.0)
        lg = jnp.where(keep, lg, 0.0)
    k_s[...] = k
    lg_s[...] = lg
    v_s[...] = _dot(h, wv[...]).astype(BF16)
    gt_s[...] = _silu(_dot(h, wr[...]))

    @pl.when(j == 0)
    def _():
        if has_s0:
            s_ref[...] = s0_ref[...]
        else:
            s_ref[...] = jnp.zeros(s_ref.shape, F32)

    ri = lax.broadcasted_iota(jnp.int32, (c, c), 0)
    ci = lax.broadcasted_iota(jnp.int32, (c, c), 1)
    causal = ri >= ci
    tri = jnp.where(causal, 1.0, 0.0).astype(BF16)
    eye = (lax.broadcasted_iota(jnp.int32, (GLA_DK, GLA_DK), 0)
           == lax.broadcasted_iota(jnp.int32, (GLA_DK, GLA_DK), 1))

    chunk_rows = [slice(ch * c, (ch + 1) * c) for ch in range(n_chunks)]
    head_k = [slice(hd * GLA_DK, (hd + 1) * GLA_DK) for hd in range(GLA_HEADS)]
    head_v = [slice(hd * GLA_DV, (hd + 1) * GLA_DV) for hd in range(GLA_HEADS)]

    for rs in chunk_rows:
        cum_s[rs, :] = _dot3_left(tri, lg_s[rs, :])
    cum = cum_s[...]
    qe_s[...] = (q_s[...] * jnp.exp(cum)).astype(BF16)
    ke_s[...] = (k_s[...] * jnp.exp(-cum)).astype(BF16)
    for ch, rs in enumerate(chunk_rows):
        last = cum_s[(ch + 1) * c - 1:(ch + 1) * c, :]
        kd_s[rs, :] = (k_s[rs, :] * jnp.exp(last - cum_s[rs, :])).astype(BF16)
    for ch, rs in enumerate(chunk_rows):
        for hd in range(GLA_HEADS):
            ks, vs = head_k[hd], head_v[hd]
            att = jnp.where(causal, _dot_nt(qe_s[rs, ks], ke_s[rs, ks]), 0.0).astype(BF16)
            vc = v_s[rs, vs]
            oi_s[rs, vs] = _dot(att, vc)
            u_s[ch, hd] = _dot_tn(kd_s[rs, ks], vc)

    for ch, rs in enumerate(chunk_rows):
        nb = ch if multi_batch else 0
        for hd in range(GLA_HEADS):
            ks, vs = head_k[hd], head_v[hd]
            s_old = s_ref[nb, hd]
            o = oi_s[rs, vs] + _dot(qe_s[rs, ks], s_old.astype(BF16))
            last = cum_s[(ch + 1) * c - 1:(ch + 1) * c, ks]
            dcol = jnp.sum(jnp.where(eye, jnp.exp(last), 0.0), axis=1, keepdims=True)
            s_ref[nb, hd] = s_old * dcol + u_s[ch, hd]
            on = o * lax.rsqrt(jnp.mean(o * o, axis=-1, keepdims=True) + RMS_EPS) * gn[...]
            og_s[rs, vs] = (on * gt_s[rs, vs]).astype(BF16)
    y = _dot(og_s[...], wo[...])
    o_ref[...] = x + _rms(y, g_ref[1:2, :])


def _gla_layer(x, s0, g2, w, *, n_batch, seq, nb, tl, valid):
    t, d = x.shape
    rows = nb * tl
    nl = seq // tl
    assert nb == 1 or nl == 1
    hk, hv = GLA_HEADS * GLA_DK, GLA_HEADS * GLA_DV
    has_s0 = s0 is not None
    st_spec = pl.BlockSpec((nb, GLA_HEADS, GLA_DK, GLA_DV), lambda i, j: (i, 0, 0, 0))
    in_specs = [pl.BlockSpec((rows, d), lambda i, j: (i * nl + j, 0))]
    args = [x]
    if has_s0:
        in_specs.append(st_spec)
        args.append(s0)
    weights = [g2, w["wq"], w["wk"], w["wv"], w["wr"], w["wg1"], w["wg2"], w["bg"], w["gn"], w["wo"]]
    in_specs += [_const_spec(a.shape) for a in weights]
    args += weights
    body = functools.partial(_gla_body, tl=tl, valid=valid, has_s0=has_s0, multi_batch=nb > 1)
    return pl.pallas_call(
        body,
        grid=(n_batch // nb, nl),
        in_specs=in_specs,
        out_specs=[pl.BlockSpec((rows, d), lambda i, j: (i * nl + j, 0)), st_spec],
        out_shape=[jax.ShapeDtypeStruct((t, d), F32),
                   jax.ShapeDtypeStruct((n_batch, GLA_HEADS, GLA_DK, GLA_DV), F32)],
        scratch_shapes=[pltpu.VMEM((rows, hk), F32), pltpu.VMEM((rows, hk), F32), pltpu.VMEM((rows, hk), F32),
                        pltpu.VMEM((rows, hv), BF16), pltpu.VMEM((rows, hv), F32), pltpu.VMEM((rows, hv), BF16),
                        pltpu.VMEM((rows, hk), F32), pltpu.VMEM((rows, hk), BF16), pltpu.VMEM((rows, hk), BF16),
                        pltpu.VMEM((rows, hk), BF16), pltpu.VMEM((rows, hv), F32),
                        pltpu.VMEM((rows // GLA_CHUNK, GLA_HEADS, GLA_DK, GLA_DV), F32)],
        compiler_params=_params(("arbitrary", "arbitrary")),
        name="gla_layer",
    )(*args)


def _fox_bias_tables():
    pq = np.zeros((3 * LANES, D_MODEL), np.float32)
    pk = np.zeros((3 * LANES, D_MODEL), np.float32)
    oq = np.zeros((1, D_MODEL), np.float32)
    ok = np.zeros((1, D_MODEL), np.float32)
    for hh in range(FOX_HEADS):
        for i in range(3):
            pq[i * LANES + hh, hh * FOX_DH + i] = 1.0
            ok[0, hh * FOX_DH + i] = 1.0
            oq[0, hh * FOX_DH + 3 + i] = 1.0
            pk[i * LANES + hh, hh * FOX_DH + 3 + i] = -1.0
    return (jnp.asarray(pq, BF16), jnp.asarray(pk, BF16), jnp.asarray(oq, F32), jnp.asarray(ok, F32))


def _fox_proj_prompt_body(x_ref, g_ref, wq, wk, wv, wf, bf, pq, pk, oq, ok, tri_ref,
                          kt_out, vt_out, lf_out, q2_out, k2_out, vb_out, carry_ref):
    tl = x_ref.shape[0]

    @pl.when(pl.program_id(1) == 0)
    def _():
        carry_ref[...] = jnp.zeros(carry_ref.shape, F32)

    h = _rms(x_ref[...], g_ref[...]).astype(BF16)
    q = (_dot(h, wq[...]) * FOX_SCALE).astype(BF16)
    k = _dot(h, wk[...])
    v = _dot(h, wv[...])
    kt_out[0] = k.T
    vt_out[0] = v.T
    vb_out[...] = v.astype(BF16)
    lf = _log_sigmoid(_dot(h, wf[...]) + bf[...])
    lf_out[...] = lf[:, :FOX_HEADS]
    cum = _dot3_left(tri_ref[...], lf) + carry_ref[...]
    carry_ref[...] = cum[tl - 1:tl, :]
    c3 = jnp.concatenate(_split3(cum), axis=1)
    qb = (_dot(c3, pq[...]) + oq[...]).astype(BF16)
    kb = (_dot(c3, pk[...]) + ok[...]).astype(BF16)
    kh = k.astype(BF16)
    for hp in range(FOX_HEADS // 2):
        a, b = hp * LANES, (hp + 1) * LANES
        q2_out[:, 2 * a:2 * a + LANES] = q[:, a:b]
        q2_out[:, 2 * a + LANES:2 * b] = qb[:, a:b]
        k2_out[:, 2 * a:2 * a + LANES] = kh[:, a:b]
        k2_out[:, 2 * a + LANES:2 * b] = kb[:, a:b]


def _fox_proj_prompt(x, g, w, tabs, *, n_batch, seq, tl):
    t, d = x.shape
    nl = seq // tl
    pq, pk, oq, ok = tabs
    tri = jnp.asarray(np.tril(np.ones((tl, tl), np.float32)), BF16)
    consts = [g, w["wq"], w["wk"], w["wv"], w["wf"], w["bf"], pq, pk, oq, ok, tri]
    row = lambda i, j: (i * nl + j, 0)
    return pl.pallas_call(
        _fox_proj_prompt_body,
        grid=(n_batch, nl),
        in_specs=[pl.BlockSpec((tl, d), row)] + [_const_spec(a.shape) for a in consts],
        out_specs=[pl.BlockSpec((1, d, tl), lambda i, j: (i, 0, j)), pl.BlockSpec((1, d, tl), lambda i, j: (i, 0, j)),
                   pl.BlockSpec((tl, FOX_HEADS), row),
                   pl.BlockSpec((tl, 2 * d), row), pl.BlockSpec((tl, 2 * d), row),
                   pl.BlockSpec((tl, d), row)],
        out_shape=[jax.ShapeDtypeStruct((n_batch, d, seq), F32), jax.ShapeDtypeStruct((n_batch, d, seq), F32),
                   jax.ShapeDtypeStruct((t, FOX_HEADS), F32),
                   jax.ShapeDtypeStruct((t, 2 * d), BF16), jax.ShapeDtypeStruct((t, 2 * d), BF16),
                   jax.ShapeDtypeStruct((t, d), BF16)],
        scratch_shapes=[pltpu.VMEM((1, LANES), F32)],
        compiler_params=_params(("arbitrary", "arbitrary")),
        name="fox_proj_prompt",
    )(x, *consts)


def _fox_attn_body(q_ref, k_ref, v_ref, o_ref, *, tq):
    qi = pl.program_id(2)
    q2 = q_ref[...]
    lane2 = lax.broadcasted_iota(jnp.int32, (1, 2 * LANES), 1) & (LANES - 1)
    sel = [jnp.where(lane2 < FOX_DH, 1.0, 0.0).astype(BF16),
           jnp.where(lane2 >= FOX_DH, 1.0, 0.0).astype(BF16)]
    ri = lax.broadcasted_iota(jnp.int32, (tq, tq), 0)
    ci = lax.broadcasted_iota(jnp.int32, (tq, tq), 1)

    def block(kj, carry, diag):
        k0 = pl.multiple_of(kj * tq, tq)
        kb = k_ref[pl.ds(k0, tq), :]
        vb = v_ref[pl.ds(k0, tq), :]
        out = []
        for hh in range(2):
            m, l, acc = carry[hh]
            s = _dot_nt(q2, kb * sel[hh])
            if diag:
                s = jnp.where(ri >= ci, s, NEG_INF)
            m_new = jnp.maximum(m, jnp.max(s, axis=1, keepdims=True))
            alpha = jnp.exp(m - m_new)
            p = jnp.exp(s - m_new)
            l = alpha * l + jnp.sum(p, axis=1, keepdims=True)
            acc = alpha * acc + _dot(p.astype(BF16), vb)
            out.append((m_new, l, acc))
        return tuple(out)

    init = tuple((jnp.full((tq, 1), NEG_INF, F32), jnp.zeros((tq, 1), F32), jnp.zeros((tq, LANES), F32))
                 for _ in range(2))
    carry = lax.fori_loop(0, qi, lambda kj, cr: block(kj, cr, False), init)
    (_, l0, a0), (_, l1, a1) = block(qi, carry, True)
    lane = lax.broadcasted_iota(jnp.int32, (1, LANES), 1)
    o_ref[...] = jnp.where(lane < FOX_DH, a0 / l0, a1 / l1).astype(BF16)


def _fox_attn(q2, k2, vb, *, n_batch, seq, tq):
    t = q2.shape[0]
    nq = seq // tq
    npair = FOX_HEADS // 2
    return pl.pallas_call(
        functools.partial(_fox_attn_body, tq=tq),
        grid=(n_batch, npair, nq),
        in_specs=[pl.BlockSpec((tq, 2 * LANES), lambda b, p, i: (b * nq + i, p)),
                  pl.BlockSpec((seq, 2 * LANES), lambda b, p, i: (b, p)),
                  pl.BlockSpec((seq, LANES), lambda b, p, i: (b, p))],
        out_specs=pl.BlockSpec((tq, LANES), lambda b, p, i: (b * nq + i, p)),
        out_shape=jax.ShapeDtypeStruct((t, D_MODEL), BF16),
        compiler_params=_params(("arbitrary", "arbitrary", "arbitrary")),
        name="fox_attn",
    )(q2, k2, vb)


def _fox_proj_sample_body(x_ref, g_ref, wq, wk, wv, wf, bf, q_out, k_out, v_out, lf_out):
    h = _rms(x_ref[...], g_ref[...]).astype(BF16)
    q_out[...] = _dot(h, wq[...]) * FOX_SCALE
    k_out[...] = _dot(h, wk[...])
    v_out[...] = _dot(h, wv[...])
    lf_out[...] = _log_sigmoid(_dot(h, wf[...]) + bf[...])[:, :FOX_HEADS]


def _fox_proj_sample(x, g, w):
    t, d = x.shape
    consts = [g, w["wq"], w["wk"], w["wv"], w["wf"], w["bf"]]
    full = lambda n: pl.BlockSpec((t, n), lambda i: (0, 0))
    return pl.pallas_call(
        _fox_proj_sample_body,
        grid=(1,),
        in_specs=[full(d)] + [_const_spec(a.shape) for a in consts],
        out_specs=[full(d), full(d), full(d), full(FOX_HEADS)],
        out_shape=[jax.ShapeDtypeStruct((t, d), F32)] * 3 + [jax.ShapeDtypeStruct((t, FOX_HEADS), F32)],
        compiler_params=_params(("arbitrary",)),
        name="fox_proj_sample",
    )(x, *consts)


def _page_suffix_body(lf_ref, out_ref):
    ji = lax.broadcasted_iota(jnp.int32, (PAGE_SIZE, PAGE_SIZE), 0)
    si = lax.broadcasted_iota(jnp.int32, (PAGE_SIZE, PAGE_SIZE), 1)
    later = jnp.where(ji > si, 1.0, 0.0).astype(BF16)
    ones = jnp.ones((PAGE_SIZE, PAGE_SIZE), BF16)
    hi, mid, lo = _split3(lf_ref[...])
    pages = out_ref.shape[0]
    sfx = _dot(hi, later) + _dot(mid, later) + _dot(lo, later)
    tot = _dot(hi, ones) + _dot(mid, ones) + _dot(lo, ones)
    out_ref[:, 0:FOX_HEADS, :] = sfx.reshape(pages, FOX_HEADS, PAGE_SIZE)
    out_ref[:, FOX_HEADS:2 * FOX_HEADS, :] = tot.reshape(pages, FOX_HEADS, PAGE_SIZE)


def _page_suffix(lf_t, pages_per_step):
    r, p = lf_t.shape
    n_pool = r // FOX_HEADS
    return pl.pallas_call(
        _page_suffix_body,
        grid=(n_pool // pages_per_step,),
        in_specs=[pl.BlockSpec((pages_per_step * FOX_HEADS, p), lambda i: (i, 0))],
        out_specs=pl.BlockSpec((pages_per_step, 2 * FOX_HEADS, p), lambda i: (i, 0, 0)),
        out_shape=jax.ShapeDtypeStruct((n_pool, 2 * FOX_HEADS, p), F32),
        compiler_params=_params(("arbitrary",)),
        name="page_suffix",
    )(lf_t)


def _fox_decode_body(pt_ref, qbd_ref, kn_ref, vn_ref, lfn_ref, *refs, n_q, pages_per_step):
    pp = pages_per_step
    page_refs = refs[:3 * pp]
    o_ref, m_s, l_s, acc_s, run_s, ccol_s = refs[3 * pp:]
    del pt_ref
    j = pl.program_id(1)
    hq = FOX_HEADS * n_q
    qbd = qbd_ref[0]

    def attend(s_parts, vt_parts):
        s = s_parts[0] if len(s_parts) == 1 else jnp.concatenate(s_parts, axis=1)
        m_old = m_s[...]
        m_new = jnp.maximum(m_old, jnp.max(s, axis=1, keepdims=True))
        alpha = jnp.exp(m_old - m_new)
        p = jnp.exp(s - m_new)
        l_s[...] = alpha * l_s[...] + jnp.sum(p, axis=1, keepdims=True)
        pb = p.astype(BF16)
        pv = _dot_nt(pb[:, 0:PAGE_SIZE], vt_parts[0])
        for i in range(1, len(vt_parts)):
            pv = pv + _dot_nt(pb[:, i * PAGE_SIZE:(i + 1) * PAGE_SIZE], vt_parts[i])
        acc_s[...] = alpha * acc_s[...] + pv
        m_s[...] = m_new

    def tile_q(bias16):
        return jnp.concatenate([bias16] * n_q, axis=0)

    @pl.when(j == 0)
    def _():
        lfn = lfn_ref[0]
        cn = lfn
        for sh in range(1, n_q):
            cn = cn + pltpu.roll(lfn, sh, axis=1)
        ccol = jnp.concatenate(
            [jnp.broadcast_to(cn[:, qq:qq + 1], (FOX_HEADS, LANES)) for qq in range(n_q)], axis=0)
        ccol_s[...] = ccol
        m_s[...] = jnp.full(m_s.shape, NEG_INF, F32)
        l_s[...] = jnp.zeros(l_s.shape, F32)
        acc_s[...] = jnp.zeros(acc_s.shape, F32)
        run_s[...] = jnp.zeros(run_s.shape, F32)
        qrow = lax.broadcasted_iota(jnp.int32, (hq, LANES), 0) >> 4
        key = lax.broadcasted_iota(jnp.int32, (hq, LANES), 1)
        s_new = _dot(qbd, kn_ref[0]) + tile_q(-cn) + ccol
        attend([jnp.where(key <= qrow, s_new, NEG_INF)], [vn_ref[0]])

    run = run_s[...]
    ccol = ccol_s[...]
    s_parts, vt_parts = [], []
    for i in range(pp):
        kt_ref, vt_ref, b_ref = page_refs[3 * i:3 * i + 3]
        bias16 = b_ref[0, 0:FOX_HEADS, :] + run
        s_parts.append(_dot(qbd, kt_ref[0].astype(BF16)) + tile_q(bias16) + ccol)
        run = run + b_ref[0, FOX_HEADS:2 * FOX_HEADS, :]
        vt_parts.append(vt_ref[0].astype(BF16))
    run_s[...] = run
    attend(s_parts, vt_parts)

    @pl.when(j == pl.num_programs(1) - 1)
    def _():
        res = acc_s[...] / l_s[...]
        hrow = lax.broadcasted_iota(jnp.int32, (FOX_HEADS, D_MODEL), 0)
        hcol = lax.broadcasted_iota(jnp.int32, (FOX_HEADS, D_MODEL), 1) >> 6
        own = hrow == hcol
        rows = [jnp.sum(jnp.where(own, res[qq * FOX_HEADS:(qq + 1) * FOX_HEADS, :], 0.0), axis=0, keepdims=True)
                for qq in range(n_q)]
        o_ref[0] = jnp.concatenate(rows, axis=0)


def _fox_decode(page_table, qbd, kt_new, vt_new, lf_new_t, kt_pool, vt_pool, bias_pool, *, n_q, pages_per_step):
    n_batch, n_pages = page_table.shape
    pp = pages_per_step
    hq = FOX_HEADS * n_q
    d = D_MODEL
    steps = n_pages // pp
    pt = page_table.reshape(-1)

    def page_map(i):
        def f(b, j, pt_ref):
            return (pt_ref[b * n_pages + (n_pages - 1 - (j * pp + i))], 0, 0)
        return f

    per_batch = lambda shape: pl.BlockSpec((1,) + shape, lambda b, j, pt_ref: (b, 0, 0))
    in_specs = [per_batch((hq, d)), per_batch((d, PAGE_SIZE)), per_batch((d, PAGE_SIZE)),
                per_batch((FOX_HEADS, LANES))]
    args = [qbd, kt_new, vt_new, lf_new_t]
    for i in range(pp):
        in_specs += [pl.BlockSpec((1, d, PAGE_SIZE), page_map(i)), pl.BlockSpec((1, d, PAGE_SIZE), page_map(i)),
                     pl.BlockSpec((1, 2 * FOX_HEADS, PAGE_SIZE), page_map(i))]
        args += [kt_pool, vt_pool, bias_pool]
    grid_spec = pltpu.PrefetchScalarGridSpec(
        num_scalar_prefetch=1,
        grid=(n_batch, steps),
        in_specs=in_specs,
        out_specs=pl.BlockSpec((1, n_q, d), lambda b, j, pt_ref: (b, 0, 0)),
        scratch_shapes=[pltpu.VMEM((hq, 1), F32), pltpu.VMEM((hq, 1), F32), pltpu.VMEM((hq, d), F32),
                        pltpu.VMEM((FOX_HEADS, LANES), F32), pltpu.VMEM((hq, LANES), F32)],
    )
    return pl.pallas_call(
        functools.partial(_fox_decode_body, n_q=n_q, pages_per_step=pp),
        grid_spec=grid_spec,
        out_shape=jax.ShapeDtypeStruct((n_batch, n_q, d), F32),
        compiler_params=_params(("arbitrary", "arbitrary")),
        name="fox_decode",
    )(pt, *args)


def _conv_tail(y, x, g_post, lng, lnb, w2, b2):
    mu = jnp.mean(y, axis=-1, keepdims=True)
    yc = y - mu
    var = jnp.mean(yc * yc, axis=-1, keepdims=True)
    yn = yc * lax.rsqrt(var + LN_EPS) * lng + lnb
    m = _dot(_silu(yn).astype(BF16), w2) + b2
    return x + _rms(m, g_post)


def _glu_in(x, g_pre, w1, b1):
    h = _rms(x, g_pre).astype(BF16)
    u2 = _dot(h, w1) + b1
    return u2[:, :D_MODEL] * jax.nn.sigmoid(u2[:, D_MODEL:])


_CONV_RB = 128


def _conv_prompt_body(x_ref, g_ref, w1, b1, wdw, bdw, lng, lnb, w2, b2, o_ref, st_ref, ext_ref, y_ref):
    tl = x_ref.shape[0]
    l = pl.program_id(1)
    x = x_ref[...]

    @pl.when(l == 0)
    def _():
        ext_ref[0:CONV_HDR, :] = jnp.zeros((CONV_HDR, D_MODEL), F32)

    ext_ref[CONV_HDR:CONV_HDR + tl, :] = _glu_in(x, g_ref[0:1, :], w1[...], b1[...])
    base = CONV_HDR - (CONV_WIDTH - 1)
    sub = 8
    for r0 in range(0, tl, _CONV_RB):
        for c0 in range(0, D_MODEL, LANES):
            cs = slice(c0, c0 + LANES)
            acc = jnp.broadcast_to(bdw[:, cs], (_CONV_RB, LANES))
            for s in range(sub):
                n = _CONV_RB if s == 0 else _CONV_RB + sub
                z = None
                for a in range((base + CONV_WIDTH - 1) // sub + 1):
                    wi = sub * a + s - base
                    if 0 <= wi < CONV_WIDTH:
                        term = ext_ref[r0 + sub * a:r0 + sub * a + n, cs] * wdw[wi:wi + 1, cs]
                        z = term if z is None else z + term
                acc = acc + z[s:s + _CONV_RB]
            y_ref[r0:r0 + _CONV_RB, cs] = acc
    o_ref[...] = _conv_tail(y_ref[...], x, g_ref[1:2, :], lng[...], lnb[...], w2[...], b2[...])

    @pl.when(l == pl.num_programs(1) - 1)
    def _():
        st_ref[0] = ext_ref[CONV_HDR + tl - (CONV_WIDTH - 1):CONV_HDR + tl, :]

    ext_ref[0:CONV_HDR, :] = ext_ref[tl:tl + CONV_HDR, :]


def _conv_prompt(x, g2, w, *, n_batch, seq, tl):
    t, d = x.shape
    nl = seq // tl
    consts = [g2, w["w1"], w["b1"], w["wdw"], w["bdw"], w["lng"], w["lnb"], w["w2"], w["b2"]]
    row = lambda i, j: (i * nl + j, 0)
    return pl.pallas_call(
        _conv_prompt_body,
        grid=(n_batch, nl),
        in_specs=[pl.BlockSpec((tl, d), row)] + [_const_spec(a.shape) for a in consts],
        out_specs=[pl.BlockSpec((tl, d), row),
                   pl.BlockSpec((1, CONV_WIDTH - 1, d), lambda i, j: (i, 0, 0))],
        out_shape=[jax.ShapeDtypeStruct((t, d), F32),
                   jax.ShapeDtypeStruct((n_batch, CONV_WIDTH - 1, d), F32)],
        scratch_shapes=[pltpu.VMEM((CONV_HDR + tl, d), F32), pltpu.VMEM((tl, d), F32)],
        compiler_params=_params(("arbitrary", "arbitrary")),
        name="conv_prompt",
    )(x, *consts)


_SAMPLE_PAD = 8


def _conv_sample_body(x_ref, st_in, g_ref, w1, b1, wdw, bdw, lng, lnb, w2, b2, o_ref, st_out, ext_ref, u_ref, y_ref,
                      *, n_q):
    n_batch = st_in.shape[0]
    x = x_ref[...]
    u_ref[...] = _glu_in(x, g_ref[0:1, :], w1[...], b1[...])
    base = CONV_HDR - (CONV_WIDTH - 1)

    def per_batch(b, carry):
        r0 = pl.multiple_of(b * _SAMPLE_PAD, _SAMPLE_PAD)
        ext_ref[0:CONV_HDR, :] = st_in[b]
        ext_ref[CONV_HDR:CONV_HDR + _SAMPLE_PAD, :] = u_ref[pl.ds(r0, _SAMPLE_PAD), :]
        acc = jnp.broadcast_to(bdw[...], (_SAMPLE_PAD, D_MODEL))
        for wi in range(CONV_WIDTH):
            acc = acc + ext_ref[base + wi:base + wi + _SAMPLE_PAD, :] * wdw[wi:wi + 1, :]
        y_ref[pl.ds(r0, _SAMPLE_PAD), :] = acc
        st_out[b] = ext_ref[base + n_q:base + n_q + CONV_WIDTH - 1, :]
        return carry

    lax.fori_loop(0, n_batch, per_batch, 0)
    o_ref[...] = _conv_tail(y_ref[...], x, g_ref[1:2, :], lng[...], lnb[...], w2[...], b2[...])


def _conv_sample(x_pad, st_pad, g2, w, *, n_q):
    t, d = x_pad.shape
    n_batch = st_pad.shape[0]
    consts = [g2, w["w1"], w["b1"], w["wdw"], w["bdw"], w["lng"], w["lnb"], w["w2"], w["b2"]]
    return pl.pallas_call(
        functools.partial(_conv_sample_body, n_q=n_q),
        grid=(1,),
        in_specs=[pl.BlockSpec((t, d), lambda i: (0, 0)),
                  pl.BlockSpec(st_pad.shape, lambda i: (0, 0, 0))] + [_const_spec(a.shape) for a in consts],
        out_specs=[pl.BlockSpec((t, d), lambda i: (0, 0)),
                   pl.BlockSpec((n_batch, CONV_WIDTH - 1, d), lambda i: (0, 0, 0))],
        out_shape=[jax.ShapeDtypeStruct((t, d), F32),
                   jax.ShapeDtypeStruct((n_batch, CONV_WIDTH - 1, d), F32)],
        scratch_shapes=[pltpu.VMEM((CONV_HDR + _SAMPLE_PAD, d), F32), pltpu.VMEM((t, d), F32),
                        pltpu.VMEM((t, d), F32)],
        compiler_params=_params(("arbitrary",)),
        name="conv_sample",
    )(x_pad, st_pad, *consts)


def _row(v):
    return v.reshape(1, -1).astype(F32)


def _pad_cols(a, n):
    return jnp.pad(a, ((0, 0), (0, n - a.shape[1])))


def _gla_weights(wq, wk, wv, wg1, wg2, bg, wr, gn, wo):
    rank = wg1.shape[1]
    return dict(wq=wq.astype(BF16), wk=wk.astype(BF16), wv=wv.astype(BF16), wr=wr.astype(BF16),
                wg1=_pad_cols(wg1, LANES).astype(BF16),
                wg2=jnp.pad(wg2, ((0, LANES - rank), (0, 0))).astype(BF16),
                bg=_row(bg), gn=_row(gn), wo=wo.astype(BF16))


def _gla_sample(xs, s0, g2, w, *, n_batch, n_q):
    d = xs.shape[1]
    c = GLA_CHUNK
    xpad = jnp.pad(xs.reshape(n_batch, n_q, d), ((0, 0), (0, c - n_q), (0, 0))).reshape(n_batch * c, d)
    out, s_fin = _gla_layer(xpad, s0, g2, w, n_batch=n_batch, seq=c, nb=8, tl=c, valid=n_q)
    return out.reshape(n_batch, c, d)[:, :n_q].reshape(n_batch * n_q, d), s_fin


def _fox_sample(xs, g_pre, w, k_pool, v_pool, lf_pool, page_table, *, n_batch, n_q):
    d = D_MODEL
    q, k_new, v_new, lf_new = _fox_proj_sample(xs, g_pre, w)
    q4 = q.reshape(n_batch, n_q, 1, FOX_HEADS, FOX_DH)
    eye = jnp.eye(FOX_HEADS, dtype=F32)[None, None, :, :, None]
    qbd = (q4 * eye).reshape(n_batch, n_q * FOX_HEADS, d).astype(BF16)
    new_t = lambda a: jnp.pad(a.reshape(n_batch, n_q, d).transpose(0, 2, 1),
                              ((0, 0), (0, 0), (0, PAGE_SIZE - n_q))).astype(BF16)
    lf_new_t = jnp.pad(lf_new.reshape(n_batch, n_q, FOX_HEADS).transpose(0, 2, 1),
                       ((0, 0), (0, 0), (0, LANES - n_q)))
    n_pool = lf_pool.shape[0]
    pool_t = lambda a: a.transpose(0, 2, 3, 1).reshape(n_pool, d, PAGE_SIZE)
    lf_t = lf_pool.transpose(0, 2, 1).reshape(n_pool * FOX_HEADS, PAGE_SIZE)
    bias_pool = _page_suffix(lf_t, pages_per_step=n_pool // 8)
    o = _fox_decode(page_table, qbd, new_t(k_new), new_t(v_new), lf_new_t,
                    pool_t(k_pool), pool_t(v_pool), bias_pool, n_q=n_q, pages_per_step=8)
    return o.reshape(n_batch * n_q, d), k_new, v_new, lf_new


def kernel(x_prompt, x_sample, state_gla, cache_fox_k, cache_fox_v, cache_fox_logf, state_conv, page_table,
           norm_g, gla_wq, gla_wk, gla_wv, gla_wg1, gla_wg2, gla_bg, gla_wr, gla_gn, gla_wo,
           fox_wq, fox_wk, fox_wv, fox_wf, fox_bf, fox_wo,
           conv_w1, conv_b1, conv_wdw, conv_bdw, conv_ln_g, conv_ln_b, conv_w2, conv_b2,
           ffn_w1, ffn_w3, ffn_w2):
    bp, seq, d = x_prompt.shape
    bs, n_q, _ = x_sample.shape
    xp = x_prompt.reshape(bp * seq, d)
    xs = x_sample.reshape(bs * n_q, d)
    tp = 512
    gla_p, gla_s, cv_p, cv_s = [], [], [], []
    fox_out = None
    for i in range(DEPTH):
        j = i // N_MIXERS
        kind = i % N_MIXERS
        g_mix = norm_g[i, 0:2].astype(F32)
        if kind == 0:
            w = _gla_weights(gla_wq[j], gla_wk[j], gla_wv[j], gla_wg1[j], gla_wg2[j], gla_bg[j], gla_wr[j],
                             gla_gn[j], gla_wo[j])
            xp, sp = _gla_layer(xp, None, g_mix, w, n_batch=bp, seq=seq, nb=1, tl=tp, valid=tp)
            xs, ss = _gla_sample(xs, state_gla[j], g_mix, w, n_batch=bs, n_q=n_q)
            gla_p.append(sp)
            gla_s.append(ss)
        elif kind == 1:
            w = dict(wq=fox_wq[j].astype(BF16), wk=fox_wk[j].astype(BF16), wv=fox_wv[j].astype(BF16),
                     wf=_pad_cols(fox_wf[j], LANES).astype(BF16), bf=_pad_cols(_row(fox_bf[j]), LANES))
            wo = fox_wo[j].astype(BF16)
            zero_b = jnp.zeros((1, d), F32)
            g_pre, g_post = g_mix[0:1], g_mix[1:2]
            ktp, vtp, lfp, q2, k2, vb = _fox_proj_prompt(xp, g_pre, w, _fox_bias_tables(), n_batch=bp, seq=seq, tl=tp)
            op = _fox_attn(q2, k2, vb, n_batch=bp, seq=seq, tq=tp)
            xp = _proj_res(xp, op, wo, zero_b, g_post, tp)
            os_, kn, vn, lfn = _fox_sample(xs, g_pre, w, cache_fox_k[j], cache_fox_v[j], cache_fox_logf[j],
                                           page_table, n_batch=bs, n_q=n_q)
            xs = _proj_res(xs, os_, wo, zero_b, g_post, bs * n_q)
            cache_view = lambda a: a.reshape(1, bp, FOX_HEADS, FOX_DH, seq).transpose(0, 1, 4, 2, 3)
            fox_out = (cache_view(ktp), cache_view(vtp),
                       lfp.reshape(1, bp, seq, FOX_HEADS),
                       kn.reshape(1, bs, n_q, FOX_HEADS, FOX_DH), vn.reshape(1, bs, n_q, FOX_HEADS, FOX_DH),
                       lfn.reshape(1, bs, n_q, FOX_HEADS))
        else:
            w = dict(w1=conv_w1[j].astype(BF16), b1=_row(conv_b1[j]), wdw=conv_wdw[j].astype(F32),
                     bdw=_row(conv_bdw[j]), lng=_row(conv_ln_g[j]), lnb=_row(conv_ln_b[j]),
                     w2=conv_w2[j].astype(BF16), b2=_row(conv_b2[j]))
            xp, stp = _conv_prompt(xp, g_mix, w, n_batch=bp, seq=seq, tl=tp)
            xs_pad = jnp.pad(xs.reshape(bs, n_q, d), ((0, 0), (0, _SAMPLE_PAD - n_q), (0, 0)))
            st_pad = jnp.pad(state_conv[j], ((0, 0), (CONV_HDR - (CONV_WIDTH - 1), 0), (0, 0)))
            xs_pad, sts = _conv_sample(xs_pad.reshape(bs * _SAMPLE_PAD, d), st_pad, g_mix, w, n_q=n_q)
            xs = xs_pad.reshape(bs, _SAMPLE_PAD, d)[:, :n_q].reshape(bs * n_q, d)
            cv_p.append(stp)
            cv_s.append(sts)
        g_ffn = norm_g[i, 2:4].astype(F32)
        w1, w3, w2 = ffn_w1[i].astype(BF16), ffn_w3[i].astype(BF16), ffn_w2[i].astype(BF16)
        xp = _ffn(xp, g_ffn, w1, w3, w2, tp)
        xs = _ffn(xs, g_ffn, w1, w3, w2, bs * n_q)
    return (xp.reshape(bp, seq, d), xs.reshape(bs, n_q, d), jnp.stack(gla_p), jnp.stack(gla_s),
            *fox_out, jnp.stack(cv_p), jnp.stack(cv_s))
```

```python
import functools

import numpy as np
import jax
import jax.numpy as jnp
from jax import lax
from jax.experimental import pallas as pl
from jax.experimental.pallas import tpu as pltpu

F32 = jnp.float32
BF16 = jnp.bfloat16

D_MODEL = 1024
DEPTH = 4
N_MIXERS = 3
GLA_HEADS = 4
GLA_DK = 128
GLA_DV = 256
GLA_TAU = 16.0
GLA_CHUNK = 64
FOX_HEADS = 16
FOX_DH = 64
FOX_SCALE = FOX_DH ** -0.5
PAGE_SIZE = 128
NEG_INF = -1e30
CONV_WIDTH = 31
CONV_HDR = 32
D_FF = 2816
RMS_EPS = 1e-6
LN_EPS = 1e-5

LANES = 128
VMEM_LIMIT = 52 * 1024 * 1024


def _dot(a, b):
    return jnp.dot(a, b, preferred_element_type=F32)


def _dot_nt(a, b):
    return lax.dot_general(a, b, (((1,), (1,)), ((), ())), preferred_element_type=F32)


def _dot_tn(a, b):
    return lax.dot_general(a, b, (((0,), (0,)), ((), ())), preferred_element_type=F32)


def _rms(x, g):
    return x * lax.rsqrt(jnp.mean(x * x, axis=-1, keepdims=True) + RMS_EPS) * g


def _silu(x):
    return x * jax.nn.sigmoid(x)


def _log_sigmoid(z):
    return jnp.minimum(z, 0.0) - jnp.log1p(jnp.exp(-jnp.abs(z)))


def _split3(a):
    hi = a.astype(BF16)
    r1 = a - hi.astype(F32)
    mid = r1.astype(BF16)
    lo = (r1 - mid.astype(F32)).astype(BF16)
    return hi, mid, lo


def _dot3_left(m, a):
    hi, mid, lo = _split3(a)
    return _dot(m, hi) + _dot(m, mid) + _dot(m, lo)


def _dot3_right(a, m):
    hi, mid, lo = _split3(a)
    return _dot(hi, m) + _dot(mid, m) + _dot(lo, m)


def _const_spec(shape):
    nd = len(shape)
    return pl.BlockSpec(shape, lambda *_: (0,) * nd, pipeline_mode=pl.Buffered(1))


def _params(sem):
    return pltpu.CompilerParams(dimension_semantics=sem, vmem_limit_bytes=VMEM_LIMIT)


_FF_CHUNKS = tuple((s, min(512, D_FF - s)) for s in range(0, D_FF, 512))


def _ffn_body(*refs, mixer_proj):
    if mixer_proj:
        x_ref, mo_ref, wo_ref, bo_ref, go_ref, g_ref, w1_ref, w3_ref, w2_ref, o_ref, acc_ref = refs
        x = x_ref[...] + _rms(_dot(mo_ref[...].astype(BF16), wo_ref[...]) + bo_ref[...], go_ref[...])
    else:
        x_ref, g_ref, w1_ref, w3_ref, w2_ref, o_ref, acc_ref = refs
        x = x_ref[...]
    h = _rms(x, g_ref[0:1, :]).astype(BF16)
    for idx, (s, n) in enumerate(_FF_CHUNKS):
        a = _dot(h, w1_ref[:, s:s + n])
        b = _dot(h, w3_ref[:, s:s + n])
        u = (_silu(a) * b).astype(BF16)
        y = _dot(u, w2_ref[s:s + n, :])
        if idx == 0:
            acc_ref[...] = y
        else:
            acc_ref[...] += y
    o_ref[...] = x + _rms(acc_ref[...], g_ref[1:2, :])


def _ffn(x, g2, w1, w3, w2, tm, mixer=None):
    t, d = x.shape
    tile = lambda n: pl.BlockSpec((tm, n), lambda i: (i, 0))
    args, in_specs = [x], [tile(d)]
    if mixer is not None:
        o, wo, bo, go = mixer
        args += [o, wo, bo, go]
        in_specs += [tile(o.shape[1]), _const_spec(wo.shape), _const_spec(bo.shape), _const_spec(go.shape)]
    consts = [g2, w1, w3, w2]
    return pl.pallas_call(
        functools.partial(_ffn_body, mixer_proj=mixer is not None),
        grid=(t // tm,),
        in_specs=in_specs + [_const_spec(a.shape) for a in consts],
        out_specs=tile(d),
        out_shape=jax.ShapeDtypeStruct((t, d), F32),
        scratch_shapes=[pltpu.VMEM((tm, d), F32)],
        compiler_params=_params(("arbitrary",)),
        name="ffn",
    )(*args, *consts)


def _gla_body(*refs, tl, valid, has_s0, multi_batch):
    if has_s0:
        x_ref, s0_ref = refs[0], refs[1]
        rest = refs[2:]
    else:
        x_ref, s0_ref = refs[0], None
        rest = refs[1:]
    (g_ref, wq, wk, wv, wr, wg1, wg2, bg, gn, wo,
     o_ref, s_ref, q_s, k_s, lg_s, v_s, gt_s, og_s, cum_s, qe_s, ke_s, kd_s, oi_s, u_s) = rest
    c = GLA_CHUNK
    rows = x_ref.shape[0]
    n_chunks = rows // c
    j = pl.program_id(1)

    x = x_ref[...]
    h = _rms(x, g_ref[0:1, :]).astype(BF16)
    q_s[...] = _dot(h, wq[...]) * (GLA_DK ** -0.5)
    k = _dot(h, wk[...])
    z = _dot(_dot(h, wg1[...]).astype(BF16), wg2[...]) + bg[...]
    lg = _log_sigmoid(z) * (1.0 / GLA_TAU)
    if valid < tl:
        assert tl & (tl - 1) == 0
        row = lax.broadcasted_iota(jnp.int32, (rows, 1), 0) & (tl - 1)
        keep = row < valid
        k = jnp.where(keep, k, 0.0)
        lg = jnp.where(keep, lg, 0.0)
    k_s[...] = k
    lg_s[...] = lg
    v_s[...] = _dot(h, wv[...]).astype(BF16)
    gt_s[...] = _silu(_dot(h, wr[...]))

    @pl.when(j == 0)
    def _():
        if has_s0:
            s_ref[...] = s0_ref[...]
        else:
            s_ref[...] = jnp.zeros(s_ref.shape, F32)

    ri = lax.broadcasted_iota(jnp.int32, (c, c), 0)
    ci = lax.broadcasted_iota(jnp.int32, (c, c), 1)
    causal = ri >= ci
    tri = jnp.where(causal, 1.0, 0.0).astype(BF16)
    eye = (lax.broadcasted_iota(jnp.int32, (GLA_DK, GLA_DK), 0)
           == lax.broadcasted_iota(jnp.int32, (GLA_DK, GLA_DK), 1))

    chunk_rows = [slice(ch * c, (ch + 1) * c) for ch in range(n_chunks)]
    head_k = [slice(hd * GLA_DK, (hd + 1) * GLA_DK) for hd in range(GLA_HEADS)]
    head_v = [slice(hd * GLA_DV, (hd + 1) * GLA_DV) for hd in range(GLA_HEADS)]

    for rs in chunk_rows:
        cum_s[rs, :] = _dot3_left(tri, lg_s[rs, :])
    cum = cum_s[...]
    qe_s[...] = (q_s[...] * jnp.exp(cum)).astype(BF16)
    ke_s[...] = (k_s[...] * jnp.exp(-cum)).astype(BF16)
    for ch, rs in enumerate(chunk_rows):
        last = cum_s[(ch + 1) * c - 1:(ch + 1) * c, :]
        kd_s[rs, :] = (k_s[rs, :] * jnp.exp(last - cum_s[rs, :])).astype(BF16)
    for ch, rs in enumerate(chunk_rows):
        for hd in range(GLA_HEADS):
            ks, vs = head_k[hd], head_v[hd]
            att = jnp.where(causal, _dot_nt(qe_s[rs, ks], ke_s[rs, ks]), 0.0).astype(BF16)
            vc = v_s[rs, vs]
            oi_s[rs, vs] = _dot(att, vc)
            u_s[ch, hd] = _dot_tn(kd_s[rs, ks], vc)

    for ch, rs in enumerate(chunk_rows):
        nb = ch if multi_batch else 0
        for hd in range(GLA_HEADS):
            ks, vs = head_k[hd], head_v[hd]
            s_old = s_ref[nb, hd]
            o = oi_s[rs, vs] + _dot(qe_s[rs, ks], s_old.astype(BF16))
            last = cum_s[(ch + 1) * c - 1:(ch + 1) * c, ks]
            dcol = jnp.sum(jnp.where(eye, jnp.exp(last), 0.0), axis=1, keepdims=True)
            s_ref[nb, hd] = s_old * dcol + u_s[ch, hd]
            on = o * lax.rsqrt(jnp.mean(o * o, axis=-1, keepdims=True) + RMS_EPS) * gn[...]
            og_s[rs, vs] = (on * gt_s[rs, vs]).astype(BF16)
    y = _dot(og_s[...], wo[...])
    o_ref[...] = x + _rms(y, g_ref[1:2, :])


def _gla_layer(x, s0, g2, w, *, n_batch, seq, nb, tl, valid):
    t, d = x.shape
    rows = nb * tl
    nl = seq // tl
    assert nb == 1 or nl == 1
    hk, hv = GLA_HEADS * GLA_DK, GLA_HEADS * GLA_DV
    has_s0 = s0 is not None
    st_spec = pl.BlockSpec((nb, GLA_HEADS, GLA_DK, GLA_DV), lambda i, j: (i, 0, 0, 0))
    in_specs = [pl.BlockSpec((rows, d), lambda i, j: (i * nl + j, 0))]
    args = [x]
    if has_s0:
        in_specs.append(st_spec)
        args.append(s0)
    weights = [g2, w["wq"], w["wk"], w["wv"], w["wr"], w["wg1"], w["wg2"], w["bg"], w["gn"], w["wo"]]
    in_specs += [_const_spec(a.shape) for a in weights]
    args += weights
    body = functools.partial(_gla_body, tl=tl, valid=valid, has_s0=has_s0, multi_batch=nb > 1)
    return pl.pallas_call(
        body,
        grid=(n_batch // nb, nl),
        in_specs=in_specs,
        out_specs=[pl.BlockSpec((rows, d), lambda i, j: (i * nl + j, 0)), st_spec],
        out_shape=[jax.ShapeDtypeStruct((t, d), F32),
                   jax.ShapeDtypeStruct((n_batch, GLA_HEADS, GLA_DK, GLA_DV), F32)],
        scratch_shapes=[pltpu.VMEM((rows, hk), F32), pltpu.VMEM((rows, hk), F32), pltpu.VMEM((rows, hk), F32),
                        pltpu.VMEM((rows, hv), BF16), pltpu.VMEM((rows, hv), F32), pltpu.VMEM((rows, hv), BF16),
                        pltpu.VMEM((rows, hk), F32), pltpu.VMEM((rows, hk), BF16), pltpu.VMEM((rows, hk), BF16),
                        pltpu.VMEM((rows, hk), BF16), pltpu.VMEM((rows, hv), F32),
                        pltpu.VMEM((rows // GLA_CHUNK, GLA_HEADS, GLA_DK, GLA_DV), F32)],
        compiler_params=_params(("arbitrary", "arbitrary")),
        name="gla_layer",
    )(*args)


def _fox_bias_tables():
    pq = np.zeros((3 * LANES, D_MODEL), np.float32)
    pk = np.zeros((3 * LANES, D_MODEL), np.float32)
    oq = np.zeros((1, D_MODEL), np.float32)
    ok = np.zeros((1, D_MODEL), np.float32)
    for hh in range(FOX_HEADS):
        for i in range(3):
            pq[i * LANES + hh, hh * FOX_DH + i] = 1.0
            ok[0, hh * FOX_DH + i] = 1.0
            oq[0, hh * FOX_DH + 3 + i] = 1.0
            pk[i * LANES + hh, hh * FOX_DH + 3 + i] = -1.0
    return (jnp.asarray(pq, BF16), jnp.asarray(pk, BF16), jnp.asarray(oq, F32), jnp.asarray(ok, F32))


def _fox_proj_prompt_body(x_ref, g_ref, wq, wk, wv, wf, bf, pq, pk, oq, ok, tri_ref,
                          kt_out, vt_out, lf_out, q2_out, k2_out, vbt_out, carry_ref):
    tl = x_ref.shape[0]

    @pl.when(pl.program_id(1) == 0)
    def _():
        carry_ref[...] = jnp.zeros(carry_ref.shape, F32)

    h = _rms(x_ref[...], g_ref[...]).astype(BF16)
    q = (_dot(h, wq[...]) * FOX_SCALE).astype(BF16)
    k = _dot(h, wk[...])
    v = _dot(h, wv[...])
    kt_out[0] = k.T
    vt = v.T
    vt_out[0] = vt
    vbt_out[0, 0] = vt.astype(BF16)
    lf = _log_sigmoid(_dot(h, wf[...]) + bf[...])
    lf_out[...] = lf[:, :FOX_HEADS]
    cum = _dot3_left(tri_ref[...], lf) + carry_ref[...]
    carry_ref[...] = cum[tl - 1:tl, :]
    c3 = jnp.concatenate(_split3(cum), axis=1)
    qb = (_dot(c3, pq[...]) + oq[...]).astype(BF16)
    kb = (_dot(c3, pk[...]) + ok[...]).astype(BF16)
    kh = k.astype(BF16)
    for hp in range(FOX_HEADS // 2):
        a, b = hp * LANES, (hp + 1) * LANES
        q2_out[:, 2 * a:2 * a + LANES] = q[:, a:b]
        q2_out[:, 2 * a + LANES:2 * b] = qb[:, a:b]
        k2_out[:, 2 * a:2 * a + LANES] = kh[:, a:b]
        k2_out[:, 2 * a + LANES:2 * b] = kb[:, a:b]


def _fox_proj_prompt(x, g, w, tabs, *, n_batch, seq, tl):
    t, d = x.shape
    nl = seq // tl
    pq, pk, oq, ok = tabs
    tri = jnp.asarray(np.tril(np.ones((tl, tl), np.float32)), BF16)
    consts = [g, w["wq"], w["wk"], w["wv"], w["wf"], w["bf"], pq, pk, oq, ok, tri]
    row = lambda i, j: (i * nl + j, 0)
    return pl.pallas_call(
        _fox_proj_prompt_body,
        grid=(n_batch, nl),
        in_specs=[pl.BlockSpec((tl, d), row)] + [_const_spec(a.shape) for a in consts],
        out_specs=[pl.BlockSpec((1, d, tl), lambda i, j: (i, 0, j)), pl.BlockSpec((1, d, tl), lambda i, j: (i, 0, j)),
                   pl.BlockSpec((tl, FOX_HEADS), row),
                   pl.BlockSpec((tl, 2 * d), row), pl.BlockSpec((tl, 2 * d), row),
                   pl.BlockSpec((1, 1, d, tl), lambda i, j: (i, j, 0, 0))],
        out_shape=[jax.ShapeDtypeStruct((n_batch, d, seq), F32), jax.ShapeDtypeStruct((n_batch, d, seq), F32),
                   jax.ShapeDtypeStruct((t, FOX_HEADS), F32),
                   jax.ShapeDtypeStruct((t, 2 * d), BF16), jax.ShapeDtypeStruct((t, 2 * d), BF16),
                   jax.ShapeDtypeStruct((n_batch, nl, d, tl), BF16)],
        scratch_shapes=[pltpu.VMEM((1, LANES), F32)],
        compiler_params=_params(("arbitrary", "arbitrary")),
        name="fox_proj_prompt",
    )(x, *consts)


def _fold8(x, op):
    n = x.shape[0]
    while n > 8:
        n //= 2
        x = op(x[:n], x[n:])
    return x


def _fox_attn_body(q_ref, k_ref, vt_ref, o_ref, s_a, s_b, p_s, m_s, l_s, a_s, acc_s, *, tq):
    nq = q_ref.shape[0] // tq
    lane2 = lax.broadcasted_iota(jnp.int32, (1, 2 * LANES), 1) & (LANES - 1)
    sel = [jnp.where(lane2 < FOX_DH, 1.0, 0.0).astype(BF16),
           jnp.where(lane2 >= FOX_DH, 1.0, 0.0).astype(BF16)]
    row = lax.broadcasted_iota(jnp.int32, (LANES, 1), 0)
    steps = [(qi, kj) for qi in range(nq) for kj in range(qi + 1)]
    stage = (s_a, s_b)

    def logits(i):
        qi, kj = steps[i]
        kb = k_ref[kj * tq:(kj + 1) * tq, :]
        for hh in range(2):
            stage[i % 2][hh] = _dot_nt(kb * sel[hh], q_ref[qi * tq:(qi + 1) * tq, :])

    def softmax_pv(i):
        qi, kj = steps[i]
        diag = kj == qi
        if kj == 0:
            m_s[...] = jnp.full(m_s.shape, NEG_INF, F32)
            l_s[...] = jnp.zeros(l_s.shape, F32)
            acc_s[...] = jnp.zeros(acc_s.shape, F32)
        vt = vt_ref[0, kj]
        for hh in range(2):
            for c0 in range(0, tq, LANES):
                cs = slice(c0, c0 + LANES)
                def load_s(r0):
                    s = stage[i % 2][hh, r0:r0 + LANES, cs]
                    if diag and r0 >= c0:
                        key = lax.broadcasted_iota(jnp.int32, (LANES, LANES), 0) + r0
                        qry = lax.broadcasted_iota(jnp.int32, (LANES, LANES), 1) + c0
                        s = jnp.where(key <= qry, s, NEG_INF)
                    return s

                mx = _fold8(load_s(0), jnp.maximum)
                for r0 in range(LANES, tq, LANES):
                    mx = jnp.maximum(mx, _fold8(load_s(r0), jnp.maximum))
                m_old = m_s[hh, :, cs]
                m_new = jnp.maximum(m_old, jnp.max(mx, axis=0, keepdims=True))
                alpha = jnp.exp(m_old - m_new)
                sm = None
                for r0 in range(0, tq, LANES):
                    p = jnp.exp(load_s(r0) - m_new)
                    part = _fold8(p, jnp.add)
                    sm = part if sm is None else sm + part
                    p_s[hh, r0:r0 + LANES, cs] = p.astype(BF16)
                l_s[hh, :, cs] = alpha * l_s[hh, :, cs] + jnp.sum(sm, axis=0, keepdims=True)
                m_s[hh, :, cs] = m_new
                a_s[hh, :, cs] = alpha
            acc_s[hh] = a_s[hh] * acc_s[hh] + _dot(vt, p_s[hh])
        if diag:
            o_t = jnp.where(row < FOX_DH, acc_s[0] / l_s[0], acc_s[1] / l_s[1])
            o_ref[qi * tq:(qi + 1) * tq, :] = o_t.T.astype(BF16)

    for i in range(len(steps)):
        logits(i)
        softmax_pv(i)


def _fox_attn(q2, k2, vbt, *, n_batch, seq, tq):
    t = q2.shape[0]
    nq = seq // tq
    npair = FOX_HEADS // 2
    return pl.pallas_call(
        functools.partial(_fox_attn_body, tq=tq),
        grid=(n_batch, npair),
        in_specs=[pl.BlockSpec((seq, 2 * LANES), lambda b, p: (b, p)),
                  pl.BlockSpec((seq, 2 * LANES), lambda b, p: (b, p)),
                  pl.BlockSpec((1, nq, LANES, tq), lambda b, p: (b, 0, p, 0))],
        out_specs=pl.BlockSpec((seq, LANES), lambda b, p: (b, p)),
        out_shape=jax.ShapeDtypeStruct((t, D_MODEL), BF16),
        scratch_shapes=[pltpu.VMEM((2, tq, tq), F32), pltpu.VMEM((2, tq, tq), F32), pltpu.VMEM((2, tq, tq), BF16),
                        pltpu.VMEM((2, 1, tq), F32), pltpu.VMEM((2, 1, tq), F32), pltpu.VMEM((2, 1, tq), F32),
                        pltpu.VMEM((2, LANES, tq), F32)],
        compiler_params=_params(("arbitrary", "arbitrary")),
        name="fox_attn",
    )(q2, k2, vbt)


def _fox_proj_sample_body(x_ref, g_ref, wq, wk, wv, wf, bf, q_out, k_out, v_out, lf_out):
    h = _rms(x_ref[...], g_ref[...]).astype(BF16)
    q_out[...] = _dot(h, wq[...]) * FOX_SCALE
    k_out[...] = _dot(h, wk[...])
    v_out[...] = _dot(h, wv[...])
    lf_out[...] = _log_sigmoid(_dot(h, wf[...]) + bf[...])[:, :FOX_HEADS]


def _fox_proj_sample(x, g, w):
    t, d = x.shape
    consts = [g, w["wq"], w["wk"], w["wv"], w["wf"], w["bf"]]
    full = lambda n: pl.BlockSpec((t, n), lambda i: (0, 0))
    return pl.pallas_call(
        _fox_proj_sample_body,
        grid=(1,),
        in_specs=[full(d)] + [_const_spec(a.shape) for a in consts],
        out_specs=[full(d), full(d), full(d), full(FOX_HEADS)],
        out_shape=[jax.ShapeDtypeStruct((t, d), F32)] * 3 + [jax.ShapeDtypeStruct((t, FOX_HEADS), F32)],
        compiler_params=_params(("arbitrary",)),
        name="fox_proj_sample",
    )(x, *consts)


def _page_suffix_body(lf_ref, out_ref):
    ji = lax.broadcasted_iota(jnp.int32, (PAGE_SIZE, PAGE_SIZE), 0)
    si = lax.broadcasted_iota(jnp.int32, (PAGE_SIZE, PAGE_SIZE), 1)
    later = jnp.where(ji > si, 1.0, 0.0).astype(BF16)
    ones = jnp.ones((PAGE_SIZE, PAGE_SIZE), BF16)
    hi, mid, lo = _split3(lf_ref[...])
    pages = out_ref.shape[0]
    sfx = _dot(hi, later) + _dot(mid, later) + _dot(lo, later)
    tot = _dot(hi, ones) + _dot(mid, ones) + _dot(lo, ones)
    out_ref[:, 0:FOX_HEADS, :] = sfx.reshape(pages, FOX_HEADS, PAGE_SIZE)
    out_ref[:, FOX_HEADS:2 * FOX_HEADS, :] = tot.reshape(pages, FOX_HEADS, PAGE_SIZE)


def _page_suffix(lf_t, pages_per_step):
    r, p = lf_t.shape
    n_pool = r // FOX_HEADS
    return pl.pallas_call(
        _page_suffix_body,
        grid=(n_pool // pages_per_step,),
        in_specs=[pl.BlockSpec((pages_per_step * FOX_HEADS, p), lambda i: (i, 0))],
        out_specs=pl.BlockSpec((pages_per_step, 2 * FOX_HEADS, p), lambda i: (i, 0, 0)),
        out_shape=jax.ShapeDtypeStruct((n_pool, 2 * FOX_HEADS, p), F32),
        compiler_params=_params(("arbitrary",)),
        name="page_suffix",
    )(lf_t)


def _fox_decode_body(pt_ref, qbd_ref, kn_ref, vn_ref, lfn_ref, *refs, n_q, pages_per_step):
    pp = pages_per_step
    page_refs = refs[:3 * pp]
    o_ref, m_s, l_s, acc_s, run_s, ccol_s = refs[3 * pp:]
    del pt_ref
    j = pl.program_id(1)
    hq = FOX_HEADS * n_q
    qbd = qbd_ref[0]

    def attend(s_parts, vt_parts):
        s = s_parts[0] if len(s_parts) == 1 else jnp.concatenate(s_parts, axis=1)
        m_old = m_s[...]
        m_new = jnp.maximum(m_old, jnp.max(s, axis=1, keepdims=True))
        alpha = jnp.exp(m_old - m_new)
        p = jnp.exp(s - m_new)
        l_s[...] = alpha * l_s[...] + jnp.sum(p, axis=1, keepdims=True)
        pb = p.astype(BF16)
        pv = _dot_nt(pb[:, 0:PAGE_SIZE], vt_parts[0])
        for i in range(1, len(vt_parts)):
            pv = pv + _dot_nt(pb[:, i * PAGE_SIZE:(i + 1) * PAGE_SIZE], vt_parts[i])
        acc_s[...] = alpha * acc_s[...] + pv
        m_s[...] = m_new

    def tile_q(bias16):
        return jnp.concatenate([bias16] * n_q, axis=0)

    @pl.when(j == 0)
    def _():
        lfn = lfn_ref[0]
        cn = lfn
        for sh in range(1, n_q):
            cn = cn + pltpu.roll(lfn, sh, axis=1)
        ccol = jnp.concatenate(
            [jnp.broadcast_to(cn[:, qq:qq + 1], (FOX_HEADS, LANES)) for qq in range(n_q)], axis=0)
        ccol_s[...] = ccol
        m_s[...] = jnp.full(m_s.shape, NEG_INF, F32)
        l_s[...] = jnp.zeros(l_s.shape, F32)
        acc_s[...] = jnp.zeros(acc_s.shape, F32)
        run_s[...] = jnp.zeros(run_s.shape, F32)
        qrow = lax.broadcasted_iota(jnp.int32, (hq, LANES), 0) >> 4
        key = lax.broadcasted_iota(jnp.int32, (hq, LANES), 1)
        s_new = _dot(qbd, kn_ref[0]) + tile_q(-cn) + ccol
        attend([jnp.where(key <= qrow, s_new, NEG_INF)], [vn_ref[0]])

    run = run_s[...]
    ccol = ccol_s[...]
    s_parts, vt_parts = [], []
    for i in range(pp):
        kt_ref, vt_ref, b_ref = page_refs[3 * i:3 * i + 3]
        bias16 = b_ref[0, 0:FOX_HEADS, :] + run
        s_parts.append(_dot(qbd, kt_ref[0].astype(BF16)) + tile_q(bias16) + ccol)
        run = run + b_ref[0, FOX_HEADS:2 * FOX_HEADS, :]
        vt_parts.append(vt_ref[0].astype(BF16))
    run_s[...] = run
    attend(s_parts, vt_parts)

    @pl.when(j == pl.num_programs(1) - 1)
    def _():
        res = acc_s[...] / l_s[...]
        hrow = lax.broadcasted_iota(jnp.int32, (FOX_HEADS, D_MODEL), 0)
        hcol = lax.broadcasted_iota(jnp.int32, (FOX_HEADS, D_MODEL), 1) >> 6
        own = hrow == hcol
        rows = [jnp.sum(jnp.where(own, res[qq * FOX_HEADS:(qq + 1) * FOX_HEADS, :], 0.0), axis=0, keepdims=True)
                for qq in range(n_q)]
        o_ref[0] = jnp.concatenate(rows, axis=0)


def _fox_decode(page_table, qbd, kt_new, vt_new, lf_new_t, kt_pool, vt_pool, bias_pool, *, n_q, pages_per_step):
    n_batch, n_pages = page_table.shape
    pp = pages_per_step
    hq = FOX_HEADS * n_q
    d = D_MODEL
    steps = n_pages // pp
    pt = page_table.reshape(-1)

    def page_map(i):
        def f(b, j, pt_ref):
            return (pt_ref[b * n_pages + (n_pages - 1 - (j * pp + i))], 0, 0)
        return f

    per_batch = lambda shape: pl.BlockSpec((1,) + shape, lambda b, j, pt_ref: (b, 0, 0))
    in_specs = [per_batch((hq, d)), per_batch((d, PAGE_SIZE)), per_batch((d, PAGE_SIZE)),
                per_batch((FOX_HEADS, LANES))]
    args = [qbd, kt_new, vt_new, lf_new_t]
    for i in range(pp):
        in_specs += [pl.BlockSpec((1, d, PAGE_SIZE), page_map(i)), pl.BlockSpec((1, d, PAGE_SIZE), page_map(i)),
                     pl.BlockSpec((1, 2 * FOX_HEADS, PAGE_SIZE), page_map(i))]
        args += [kt_pool, vt_pool, bias_pool]
    grid_spec = pltpu.PrefetchScalarGridSpec(
        num_scalar_prefetch=1,
        grid=(n_batch, steps),
        in_specs=in_specs,
        out_specs=pl.BlockSpec((1, n_q, d), lambda b, j, pt_ref: (b, 0, 0)),
        scratch_shapes=[pltpu.VMEM((hq, 1), F32), pltpu.VMEM((hq, 1), F32), pltpu.VMEM((hq, d), F32),
                        pltpu.VMEM((FOX_HEADS, LANES), F32), pltpu.VMEM((hq, LANES), F32)],
    )
    return pl.pallas_call(
        functools.partial(_fox_decode_body, n_q=n_q, pages_per_step=pp),
        grid_spec=grid_spec,
        out_shape=jax.ShapeDtypeStruct((n_batch, n_q, d), F32),
        compiler_params=_params(("arbitrary", "arbitrary")),
        name="fox_decode",
    )(pt, *args)


def _conv_tail(y, x, g_post, lng, lnb, w2, b2):
    mu = jnp.mean(y, axis=-1, keepdims=True)
    yc = y - mu
    var = jnp.mean(yc * yc, axis=-1, keepdims=True)
    yn = yc * lax.rsqrt(var + LN_EPS) * lng + lnb
    m = _dot(_silu(yn).astype(BF16), w2) + b2
    return x + _rms(m, g_post)


def _glu_in(x, g_pre, w1, b1):
    h = _rms(x, g_pre).astype(BF16)
    u2 = _dot(h, w1) + b1
    return u2[:, :D_MODEL] * jax.nn.sigmoid(u2[:, D_MODEL:])


_CONV_RB = 128


def _conv_prompt_body(x_ref, g_ref, w1, b1, wdw, bdw, lng, lnb, w2, b2, o_ref, st_ref, ext_ref, y_ref):
    tl = x_ref.shape[0]
    l = pl.program_id(1)
    x = x_ref[...]

    @pl.when(l == 0)
    def _():
        ext_ref[0:CONV_HDR, :] = jnp.zeros((CONV_HDR, D_MODEL), F32)

    ext_ref[CONV_HDR:CONV_HDR + tl, :] = _glu_in(x, g_ref[0:1, :], w1[...], b1[...])
    base = CONV_HDR - (CONV_WIDTH - 1)
    sub = 8
    for r0 in range(0, tl, _CONV_RB):
        for c0 in range(0, D_MODEL, LANES):
            cs = slice(c0, c0 + LANES)
            acc = jnp.broadcast_to(bdw[:, cs], (_CONV_RB, LANES))
            for s in range(sub):
                n = _CONV_RB if s == 0 else _CONV_RB + sub
                z = None
                for a in range((base + CONV_WIDTH - 1) // sub + 1):
                    wi = sub * a + s - base
                    if 0 <= wi < CONV_WIDTH:
                        term = ext_ref[r0 + sub * a:r0 + sub * a + n, cs] * wdw[wi:wi + 1, cs]
                        z = term if z is None else z + term
                acc = acc + z[s:s + _CONV_RB]
            y_ref[r0:r0 + _CONV_RB, cs] = acc
    o_ref[...] = _conv_tail(y_ref[...], x, g_ref[1:2, :], lng[...], lnb[...], w2[...], b2[...])

    @pl.when(l == pl.num_programs(1) - 1)
    def _():
        st_ref[0] = ext_ref[CONV_HDR + tl - (CONV_WIDTH - 1):CONV_HDR + tl, :]

    ext_ref[0:CONV_HDR, :] = ext_ref[tl:tl + CONV_HDR, :]


def _conv_prompt(x, g2, w, *, n_batch, seq, tl):
    t, d = x.shape
    nl = seq // tl
    consts = [g2, w["w1"], w["b1"], w["wdw"], w["bdw"], w["lng"], w["lnb"], w["w2"], w["b2"]]
    row = lambda i, j: (i * nl + j, 0)
    return pl.pallas_call(
        _conv_prompt_body,
        grid=(n_batch, nl),
        in_specs=[pl.BlockSpec((tl, d), row)] + [_const_spec(a.shape) for a in consts],
        out_specs=[pl.BlockSpec((tl, d), row),
                   pl.BlockSpec((1, CONV_WIDTH - 1, d), lambda i, j: (i, 0, 0))],
        out_shape=[jax.ShapeDtypeStruct((t, d), F32),
                   jax.ShapeDtypeStruct((n_batch, CONV_WIDTH - 1, d), F32)],
        scratch_shapes=[pltpu.VMEM((CONV_HDR + tl, d), F32), pltpu.VMEM((tl, d), F32)],
        compiler_params=_params(("arbitrary", "arbitrary")),
        name="conv_prompt",
    )(x, *consts)


_SAMPLE_PAD = 8


def _conv_sample_body(x_ref, st_in, g_ref, w1, b1, wdw, bdw, lng, lnb, w2, b2, o_ref, st_out, ext_ref, u_ref, y_ref,
                      *, n_q):
    n_batch = st_in.shape[0]
    x = x_ref[...]
    u_ref[...] = _glu_in(x, g_ref[0:1, :], w1[...], b1[...])
    base = CONV_HDR - (CONV_WIDTH - 1)

    def per_batch(b, carry):
        r0 = pl.multiple_of(b * _SAMPLE_PAD, _SAMPLE_PAD)
        ext_ref[0:CONV_HDR, :] = st_in[b]
        ext_ref[CONV_HDR:CONV_HDR + _SAMPLE_PAD, :] = u_ref[pl.ds(r0, _SAMPLE_PAD), :]
        acc = jnp.broadcast_to(bdw[...], (_SAMPLE_PAD, D_MODEL))
        for wi in range(CONV_WIDTH):
            acc = acc + ext_ref[base + wi:base + wi + _SAMPLE_PAD, :] * wdw[wi:wi + 1, :]
        y_ref[pl.ds(r0, _SAMPLE_PAD), :] = acc
        st_out[b] = ext_ref[base + n_q:base + n_q + CONV_WIDTH - 1, :]
        return carry

    lax.fori_loop(0, n_batch, per_batch, 0)
    o_ref[...] = _conv_tail(y_ref[...], x, g_ref[1:2, :], lng[...], lnb[...], w2[...], b2[...])


def _conv_sample(x_pad, st_pad, g2, w, *, n_q):
    t, d = x_pad.shape
    n_batch = st_pad.shape[0]
    consts = [g2, w["w1"], w["b1"], w["wdw"], w["bdw"], w["lng"], w["lnb"], w["w2"], w["b2"]]
    return pl.pallas_call(
        functools.partial(_conv_sample_body, n_q=n_q),
        grid=(1,),
        in_specs=[pl.BlockSpec((t, d), lambda i: (0, 0)),
                  pl.BlockSpec(st_pad.shape, lambda i: (0, 0, 0))] + [_const_spec(a.shape) for a in consts],
        out_specs=[pl.BlockSpec((t, d), lambda i: (0, 0)),
                   pl.BlockSpec((n_batch, CONV_WIDTH - 1, d), lambda i: (0, 0, 0))],
        out_shape=[jax.ShapeDtypeStruct((t, d), F32),
                   jax.ShapeDtypeStruct((n_batch, CONV_WIDTH - 1, d), F32)],
        scratch_shapes=[pltpu.VMEM((CONV_HDR + _SAMPLE_PAD, d), F32), pltpu.VMEM((t, d), F32),
                        pltpu.VMEM((t, d), F32)],
        compiler_params=_params(("arbitrary",)),
        name="conv_sample",
    )(x_pad, st_pad, *consts)


def _row(v):
    return v.reshape(1, -1).astype(F32)


def _pad_cols(a, n):
    return jnp.pad(a, ((0, 0), (0, n - a.shape[1])))


def _gla_weights(wq, wk, wv, wg1, wg2, bg, wr, gn, wo):
    rank = wg1.shape[1]
    return dict(wq=wq.astype(BF16), wk=wk.astype(BF16), wv=wv.astype(BF16), wr=wr.astype(BF16),
                wg1=_pad_cols(wg1, LANES).astype(BF16),
                wg2=jnp.pad(wg2, ((0, LANES - rank), (0, 0))).astype(BF16),
                bg=_row(bg), gn=_row(gn), wo=wo.astype(BF16))


def _gla_sample(xs, s0, g2, w, *, n_batch, n_q):
    d = xs.shape[1]
    c = GLA_CHUNK
    xpad = jnp.pad(xs.reshape(n_batch, n_q, d), ((0, 0), (0, c - n_q), (0, 0))).reshape(n_batch * c, d)
    out, s_fin = _gla_layer(xpad, s0, g2, w, n_batch=n_batch, seq=c, nb=8, tl=c, valid=n_q)
    return out.reshape(n_batch, c, d)[:, :n_q].reshape(n_batch * n_q, d), s_fin


def _fox_sample(xs, g_pre, w, k_pool, v_pool, lf_pool, page_table, *, n_batch, n_q):
    d = D_MODEL
    q, k_new, v_new, lf_new = _fox_proj_sample(xs, g_pre, w)
    q4 = q.reshape(n_batch, n_q, 1, FOX_HEADS, FOX_DH)
    eye = jnp.eye(FOX_HEADS, dtype=F32)[None, None, :, :, None]
    qbd = (q4 * eye).reshape(n_batch, n_q * FOX_HEADS, d).astype(BF16)
    new_t = lambda a: jnp.pad(a.reshape(n_batch, n_q, d).transpose(0, 2, 1),
                              ((0, 0), (0, 0), (0, PAGE_SIZE - n_q))).astype(BF16)
    lf_new_t = jnp.pad(lf_new.reshape(n_batch, n_q, FOX_HEADS).transpose(0, 2, 1),
                       ((0, 0), (0, 0), (0, LANES - n_q)))
    n_pool = lf_pool.shape[0]
    pool_t = lambda a: a.transpose(0, 2, 3, 1).reshape(n_pool, d, PAGE_SIZE)
    lf_t = lf_pool.transpose(0, 2, 1).reshape(n_pool * FOX_HEADS, PAGE_SIZE)
    bias_pool = _page_suffix(lf_t, pages_per_step=n_pool // 8)
    o = _fox_decode(page_table, qbd, new_t(k_new), new_t(v_new), lf_new_t,
                    pool_t(k_pool), pool_t(v_pool), bias_pool, n_q=n_q, pages_per_step=8)
    return o.reshape(n_batch * n_q, d), k_new, v_new, lf_new


def kernel(x_prompt, x_sample, state_gla, cache_fox_k, cache_fox_v, cache_fox_logf, state_conv, page_table,
           norm_g, gla_wq, gla_wk, gla_wv, gla_wg1, gla_wg2, gla_bg, gla_wr, gla_gn, gla_wo,
           fox_wq, fox_wk, fox_wv, fox_wf, fox_bf, fox_wo,
           conv_w1, conv_b1, conv_wdw, conv_bdw, conv_ln_g, conv_ln_b, conv_w2, conv_b2,
           ffn_w1, ffn_w3, ffn_w2):
    bp, seq, d = x_prompt.shape
    bs, n_q, _ = x_sample.shape
    xp = x_prompt.reshape(bp * seq, d)
    xs = x_sample.reshape(bs * n_q, d)
    tp = 512
    gla_p, gla_s, cv_p, cv_s = [], [], [], []
    fox_out = None
    for i in range(DEPTH):
        j = i // N_MIXERS
        kind = i % N_MIXERS
        g_mix = norm_g[i, 0:2].astype(F32)
        mix_p = mix_s = None
        if kind == 0:
            w = _gla_weights(gla_wq[j], gla_wk[j], gla_wv[j], gla_wg1[j], gla_wg2[j], gla_bg[j], gla_wr[j],
                             gla_gn[j], gla_wo[j])
            xp, sp = _gla_layer(xp, None, g_mix, w, n_batch=bp, seq=seq, nb=1, tl=tp, valid=tp)
            xs, ss = _gla_sample(xs, state_gla[j], g_mix, w, n_batch=bs, n_q=n_q)
            gla_p.append(sp)
            gla_s.append(ss)
        elif kind == 1:
            w = dict(wq=fox_wq[j].astype(BF16), wk=fox_wk[j].astype(BF16), wv=fox_wv[j].astype(BF16),
                     wf=_pad_cols(fox_wf[j], LANES).astype(BF16), bf=_pad_cols(_row(fox_bf[j]), LANES))
            wo = fox_wo[j].astype(BF16)
            zero_b = jnp.zeros((1, d), F32)
            g_pre, g_post = g_mix[0:1], g_mix[1:2]
            ktp, vtp, lfp, q2, k2, vbt = _fox_proj_prompt(xp, g_pre, w, _fox_bias_tables(), n_batch=bp, seq=seq, tl=tp)
            op = _fox_attn(q2, k2, vbt, n_batch=bp, seq=seq, tq=tp)
            os_, kn, vn, lfn = _fox_sample(xs, g_pre, w, cache_fox_k[j], cache_fox_v[j], cache_fox_logf[j],
                                           page_table, n_batch=bs, n_q=n_q)
            mix_p, mix_s = (op, wo, zero_b, g_post), (os_, wo, zero_b, g_post)
            cache_view = lambda a: a.reshape(1, bp, FOX_HEADS, FOX_DH, seq).transpose(0, 1, 4, 2, 3)
            fox_out = (cache_view(ktp), cache_view(vtp),
                       lfp.reshape(1, bp, seq, FOX_HEADS),
                       kn.reshape(1, bs, n_q, FOX_HEADS, FOX_DH), vn.reshape(1, bs, n_q, FOX_HEADS, FOX_DH),
                       lfn.reshape(1, bs, n_q, FOX_HEADS))
        else:
            w = dict(w1=conv_w1[j].astype(BF16), b1=_row(conv_b1[j]), wdw=conv_wdw[j].astype(F32),
                     bdw=_row(conv_bdw[j]), lng=_row(conv_ln_g[j]), lnb=_row(conv_ln_b[j]),
                     w2=conv_w2[j].astype(BF16), b2=_row(conv_b2[j]))
            xp, stp = _conv_prompt(xp, g_mix, w, n_batch=bp, seq=seq, tl=tp)
            xs_pad = jnp.pad(xs.reshape(bs, n_q, d), ((0, 0), (0, _SAMPLE_PAD - n_q), (0, 0)))
            st_pad = jnp.pad(state_conv[j], ((0, 0), (CONV_HDR - (CONV_WIDTH - 1), 0), (0, 0)))
            xs_pad, sts = _conv_sample(xs_pad.reshape(bs * _SAMPLE_PAD, d), st_pad, g_mix, w, n_q=n_q)
            xs = xs_pad.reshape(bs, _SAMPLE_PAD, d)[:, :n_q].reshape(bs * n_q, d)
            cv_p.append(stp)
            cv_s.append(sts)
        g_ffn = norm_g[i, 2:4].astype(F32)
        w1, w3, w2 = ffn_w1[i].astype(BF16), ffn_w3[i].astype(BF16), ffn_w2[i].astype(BF16)
        xp = _ffn(xp, g_ffn, w1, w3, w2, tp, mixer=mix_p)
        xs = _ffn(xs, g_ffn, w1, w3, w2, bs * n_q, mixer=mix_s)
    return (xp.reshape(bp, seq, d), xs.reshape(bs, n_q, d), jnp.stack(gla_p), jnp.stack(gla_s),
            *fox_out, jnp.stack(cv_p), jnp.stack(cv_s))
```

```python
import functools

import numpy as np
import jax
import jax.numpy as jnp
from jax import lax
from jax.experimental import pallas as pl
from jax.experimental.pallas import tpu as pltpu

F32 = jnp.float32
BF16 = jnp.bfloat16

D_MODEL = 1024
DEPTH = 4
N_MIXERS = 3
GLA_HEADS = 4
GLA_DK = 128
GLA_DV = 256
GLA_TAU = 16.0
GLA_CHUNK = 64
FOX_HEADS = 16
FOX_DH = 64
FOX_SCALE = FOX_DH ** -0.5
LOG2E = 1.4426950408889634
PAGE_SIZE = 128
NEG_INF = -1e30
CONV_WIDTH = 31
CONV_HDR = 32
D_FF = 2816
RMS_EPS = 1e-6
LN_EPS = 1e-5

LANES = 128
VMEM_LIMIT = 52 * 1024 * 1024


def _dot(a, b):
    return jnp.dot(a, b, preferred_element_type=F32)


def _dot_nt(a, b):
    return lax.dot_general(a, b, (((1,), (1,)), ((), ())), preferred_element_type=F32)


def _dot_tn(a, b):
    return lax.dot_general(a, b, (((0,), (0,)), ((), ())), preferred_element_type=F32)


def _rms(x, g):
    return x * lax.rsqrt(jnp.mean(x * x, axis=-1, keepdims=True) + RMS_EPS) * g


def _silu(x):
    return x * jax.nn.sigmoid(x)


def _log_sigmoid(z):
    return jnp.minimum(z, 0.0) - jnp.log1p(jnp.exp(-jnp.abs(z)))


def _split3(a):
    hi = a.astype(BF16)
    r1 = a - hi.astype(F32)
    mid = r1.astype(BF16)
    lo = (r1 - mid.astype(F32)).astype(BF16)
    return hi, mid, lo


def _dot3_left(m, a):
    hi, mid, lo = _split3(a)
    return _dot(m, hi) + _dot(m, mid) + _dot(m, lo)


def _dot3_right(a, m):
    hi, mid, lo = _split3(a)
    return _dot(hi, m) + _dot(mid, m) + _dot(lo, m)


def _const_spec(shape):
    nd = len(shape)
    return pl.BlockSpec(shape, lambda *_: (0,) * nd, pipeline_mode=pl.Buffered(1))


def _params(sem):
    return pltpu.CompilerParams(dimension_semantics=sem, vmem_limit_bytes=VMEM_LIMIT)


_FF_CHUNKS = tuple((s, min(512, D_FF - s)) for s in range(0, D_FF, 512))


def _ffn_body(*refs, mixer_proj):
    if mixer_proj:
        x_ref, mo_ref, wo_ref, bo_ref, go_ref, g_ref, w1_ref, w3_ref, w2_ref, o_ref, acc_ref = refs
        x = x_ref[...] + _rms(_dot(mo_ref[...].astype(BF16), wo_ref[...]) + bo_ref[...], go_ref[...])
    else:
        x_ref, g_ref, w1_ref, w3_ref, w2_ref, o_ref, acc_ref = refs
        x = x_ref[...]
    h = _rms(x, g_ref[0:1, :]).astype(BF16)
    for idx, (s, n) in enumerate(_FF_CHUNKS):
        a = _dot(h, w1_ref[0, :, s:s + n])
        b = _dot(h, w3_ref[0, :, s:s + n])
        u = (_silu(a) * b).astype(BF16)
        y = _dot(u, w2_ref[0, s:s + n, :])
        if idx == 0:
            acc_ref[...] = y
        else:
            acc_ref[...] += y
    o_ref[...] = x + _rms(acc_ref[...], g_ref[1:2, :])


def _ffn(x, g2, w1, w3, w2, layer, tm, mixer=None):
    t, d = x.shape
    tile = lambda n: pl.BlockSpec((tm, n), lambda i: (i, 0))
    args, in_specs = [x], [tile(d)]
    if mixer is not None:
        o, wo, bo, go = mixer
        args += [o, wo, bo, go]
        in_specs += [tile(o.shape[1]), _const_spec(wo.shape), _const_spec(bo.shape), _const_spec(go.shape)]
    consts = [g2, w1, w3, w2]
    layer_spec = lambda a: pl.BlockSpec((1,) + a.shape[1:], lambda i: (layer, 0, 0), pipeline_mode=pl.Buffered(1))
    return pl.pallas_call(
        functools.partial(_ffn_body, mixer_proj=mixer is not None),
        grid=(t // tm,),
        in_specs=in_specs + [_const_spec(g2.shape), layer_spec(w1), layer_spec(w3), layer_spec(w2)],
        out_specs=tile(d),
        out_shape=jax.ShapeDtypeStruct((t, d), F32),
        scratch_shapes=[pltpu.VMEM((tm, d), F32)],
        compiler_params=_params(("arbitrary",)),
        name="ffn",
    )(*args, *consts)


def _gla_body(*refs, tl, valid, has_s0, multi_batch):
    if has_s0:
        x_ref, s0_ref = refs[0], refs[1]
        rest = refs[2:]
    else:
        x_ref, s0_ref = refs[0], None
        rest = refs[1:]
    (g_ref, wq, wk, wv, wr, wg1, wg2, bg, gn, wo,
     o_ref, s_ref, q_s, k_s, lg_s, v_s, gt_s, og_s, cum_s, qe_s, ke_s, kd_s, oi_s, u_s) = rest
    c = GLA_CHUNK
    rows = x_ref.shape[0]
    n_chunks = rows // c
    j = pl.program_id(1)

    x = x_ref[...]
    h = _rms(x, g_ref[0:1, :]).astype(BF16)
    q_s[...] = _dot(h, wq[...]) * (GLA_DK ** -0.5)
    k = _dot(h, wk[...])
    z = _dot(_dot(h, wg1[...]).astype(BF16), wg2[...]) + bg[...]
    lg = _log_sigmoid(z) * (1.0 / GLA_TAU)
    if valid < tl:
        assert tl & (tl - 1) == 0
        row = lax.broadcasted_iota(jnp.int32, (rows, 1), 0) & (tl - 1)
        keep = row < valid
        k = jnp.where(keep, k, 0.0)
        lg = jnp.where(keep, lg, 0.0)
    k_s[...] = k
    lg_s[...] = lg
    v_s[...] = _dot(h, wv[...]).astype(BF16)
    gt_s[...] = _silu(_dot(h, wr[...]))

    @pl.when(j == 0)
    def _():
        if has_s0:
            s_ref[...] = s0_ref[0]
        else:
            s_ref[...] = jnp.zeros(s_ref.shape, F32)

    ri = lax.broadcasted_iota(jnp.int32, (c, c), 0)
    ci = lax.broadcasted_iota(jnp.int32, (c, c), 1)
    causal = ri >= ci
    tri = jnp.where(causal, 1.0, 0.0).astype(BF16)
    eye = (lax.broadcasted_iota(jnp.int32, (GLA_DK, GLA_DK), 0)
           == lax.broadcasted_iota(jnp.int32, (GLA_DK, GLA_DK), 1))

    chunk_rows = [slice(ch * c, (ch + 1) * c) for ch in range(n_chunks)]
    head_k = [slice(hd * GLA_DK, (hd + 1) * GLA_DK) for hd in range(GLA_HEADS)]
    head_v = [slice(hd * GLA_DV, (hd + 1) * GLA_DV) for hd in range(GLA_HEADS)]

    for rs in chunk_rows:
        cum_s[rs, :] = _dot3_left(tri, lg_s[rs, :])
    cum = cum_s[...]
    qe_s[...] = (q_s[...] * jnp.exp(cum)).astype(BF16)
    ke_s[...] = (k_s[...] * jnp.exp(-cum)).astype(BF16)
    for ch, rs in enumerate(chunk_rows):
        last = cum_s[(ch + 1) * c - 1:(ch + 1) * c, :]
        kd_s[rs, :] = (k_s[rs, :] * jnp.exp(last - cum_s[rs, :])).astype(BF16)
    for ch, rs in enumerate(chunk_rows):
        for hd in range(GLA_HEADS):
            ks, vs = head_k[hd], head_v[hd]
            att = jnp.where(causal, _dot_nt(qe_s[rs, ks], ke_s[rs, ks]), 0.0).astype(BF16)
            vc = v_s[rs, vs]
            oi_s[rs, vs] = _dot(att, vc)
            u_s[ch, hd] = _dot_tn(kd_s[rs, ks], vc)

    for ch, rs in enumerate(chunk_rows):
        nb = ch if multi_batch else 0
        for hd in range(GLA_HEADS):
            ks, vs = head_k[hd], head_v[hd]
            s_old = s_ref[nb, hd]
            o = oi_s[rs, vs] + _dot(qe_s[rs, ks], s_old.astype(BF16))
            last = cum_s[(ch + 1) * c - 1:(ch + 1) * c, ks]
            dcol = jnp.sum(jnp.where(eye, jnp.exp(last), 0.0), axis=1, keepdims=True)
            s_ref[nb, hd] = s_old * dcol + u_s[ch, hd]
            on = o * lax.rsqrt(jnp.mean(o * o, axis=-1, keepdims=True) + RMS_EPS) * gn[...]
            og_s[rs, vs] = (on * gt_s[rs, vs]).astype(BF16)
    y = _dot(og_s[...], wo[...])
    o_ref[...] = x + _rms(y, g_ref[1:2, :])


def _gla_layer(x, s0, g2, w, *, n_batch, seq, nb, tl, valid):
    t, d = x.shape
    rows = nb * tl
    nl = seq // tl
    assert nb == 1 or nl == 1
    hk, hv = GLA_HEADS * GLA_DK, GLA_HEADS * GLA_DV
    has_s0 = s0 is not None
    st_spec = pl.BlockSpec((nb, GLA_HEADS, GLA_DK, GLA_DV), lambda i, j: (i, 0, 0, 0))
    in_specs = [pl.BlockSpec((rows, d), lambda i, j: (i * nl + j, 0))]
    args = [x]
    if has_s0:
        s0_all, s0_layer = s0
        in_specs.append(pl.BlockSpec((1, nb, GLA_HEADS, GLA_DK, GLA_DV), lambda i, j: (s0_layer, i, 0, 0, 0)))
        args.append(s0_all)
    weights = [g2, w["wq"], w["wk"], w["wv"], w["wr"], w["wg1"], w["wg2"], w["bg"], w["gn"], w["wo"]]
    in_specs += [_const_spec(a.shape) for a in weights]
    args += weights
    body = functools.partial(_gla_body, tl=tl, valid=valid, has_s0=has_s0, multi_batch=nb > 1)
    return pl.pallas_call(
        body,
        grid=(n_batch // nb, nl),
        in_specs=in_specs,
        out_specs=[pl.BlockSpec((rows, d), lambda i, j: (i * nl + j, 0)), st_spec],
        out_shape=[jax.ShapeDtypeStruct((t, d), F32),
                   jax.ShapeDtypeStruct((n_batch, GLA_HEADS, GLA_DK, GLA_DV), F32)],
        scratch_shapes=[pltpu.VMEM((rows, hk), F32), pltpu.VMEM((rows, hk), F32), pltpu.VMEM((rows, hk), F32),
                        pltpu.VMEM((rows, hv), BF16), pltpu.VMEM((rows, hv), F32), pltpu.VMEM((rows, hv), BF16),
                        pltpu.VMEM((rows, hk), F32), pltpu.VMEM((rows, hk), BF16), pltpu.VMEM((rows, hk), BF16),
                        pltpu.VMEM((rows, hk), BF16), pltpu.VMEM((rows, hv), F32),
                        pltpu.VMEM((rows // GLA_CHUNK, GLA_HEADS, GLA_DK, GLA_DV), F32)],
        compiler_params=_params(("arbitrary", "arbitrary")),
        name="gla_layer",
    )(*args)


def _fox_bias_tables():
    pq = np.zeros((3 * LANES, D_MODEL), np.float32)
    pk = np.zeros((3 * LANES, D_MODEL), np.float32)
    oq = np.zeros((1, D_MODEL), np.float32)
    ok = np.zeros((1, D_MODEL), np.float32)
    for hh in range(FOX_HEADS):
        for i in range(3):
            pq[i * LANES + hh, hh * FOX_DH + i] = 1.0
            ok[0, hh * FOX_DH + i] = 1.0
            oq[0, hh * FOX_DH + 3 + i] = 1.0
            pk[i * LANES + hh, hh * FOX_DH + 3 + i] = -1.0
    return (jnp.asarray(pq, BF16), jnp.asarray(pk, BF16), jnp.asarray(oq, F32), jnp.asarray(ok, F32))


def _fox_proj_prompt_body(x_ref, g_ref, wq, wk, wv, wf, bf, pq, pk, oq, ok, tri_ref,
                          kt_out, vt_out, lf_out, q2_out, k2_out, vbt_out, carry_ref):
    tl = x_ref.shape[0]

    @pl.when(pl.program_id(1) == 0)
    def _():
        carry_ref[...] = jnp.zeros(carry_ref.shape, F32)

    h = _rms(x_ref[...], g_ref[...]).astype(BF16)
    q = (_dot(h, wq[...]) * (FOX_SCALE * LOG2E)).astype(BF16)
    k = _dot(h, wk[...])
    v = _dot(h, wv[...])
    kt_out[0] = k.T
    vt = v.T
    vt_out[0] = vt
    vbt_out[0, 0] = vt.astype(BF16)
    lf = _log_sigmoid(_dot(h, wf[...]) + bf[...])
    lf_out[...] = lf[:, :FOX_HEADS]
    cum = _dot3_left(tri_ref[...], lf) + carry_ref[...]
    carry_ref[...] = cum[tl - 1:tl, :]
    c3 = jnp.concatenate(_split3(cum * LOG2E), axis=1)
    qb = (_dot(c3, pq[...]) + oq[...]).astype(BF16)
    kb = (_dot(c3, pk[...]) + ok[...]).astype(BF16)
    kh = k.astype(BF16)
    for hp in range(FOX_HEADS // 2):
        a, b = hp * LANES, (hp + 1) * LANES
        q2_out[:, 2 * a:2 * a + LANES] = q[:, a:b]
        q2_out[:, 2 * a + LANES:2 * b] = qb[:, a:b]
        k2_out[:, 2 * a:2 * a + LANES] = kh[:, a:b]
        k2_out[:, 2 * a + LANES:2 * b] = kb[:, a:b]


def _fox_proj_prompt(x, g, w, tabs, *, n_batch, seq, tl):
    t, d = x.shape
    nl = seq // tl
    pq, pk, oq, ok = tabs
    tri = jnp.asarray(np.tril(np.ones((tl, tl), np.float32)), BF16)
    consts = [g, w["wq"], w["wk"], w["wv"], w["wf"], w["bf"], pq, pk, oq, ok, tri]
    row = lambda i, j: (i * nl + j, 0)
    return pl.pallas_call(
        _fox_proj_prompt_body,
        grid=(n_batch, nl),
        in_specs=[pl.BlockSpec((tl, d), row)] + [_const_spec(a.shape) for a in consts],
        out_specs=[pl.BlockSpec((1, d, tl), lambda i, j: (i, 0, j)), pl.BlockSpec((1, d, tl), lambda i, j: (i, 0, j)),
                   pl.BlockSpec((tl, FOX_HEADS), row),
                   pl.BlockSpec((tl, 2 * d), row), pl.BlockSpec((tl, 2 * d), row),
                   pl.BlockSpec((1, 1, d, tl), lambda i, j: (i, j, 0, 0))],
        out_shape=[jax.ShapeDtypeStruct((n_batch, d, seq), F32), jax.ShapeDtypeStruct((n_batch, d, seq), F32),
                   jax.ShapeDtypeStruct((t, FOX_HEADS), F32),
                   jax.ShapeDtypeStruct((t, 2 * d), BF16), jax.ShapeDtypeStruct((t, 2 * d), BF16),
                   jax.ShapeDtypeStruct((n_batch, nl, d, tl), BF16)],
        scratch_shapes=[pltpu.VMEM((1, LANES), F32)],
        compiler_params=_params(("arbitrary", "arbitrary")),
        name="fox_proj_prompt",
    )(x, *consts)


def _fold8(x, op):
    n = x.shape[0]
    while n > 8:
        n //= 2
        x = op(x[:n], x[n:])
    return x


def _fox_attn_body(q_ref, k_ref, vt_ref, o_ref, s_a, s_b, p_s, m_s, l_s, a_s, acc_s, *, tq):
    nq = q_ref.shape[0] // tq
    lane2 = lax.broadcasted_iota(jnp.int32, (1, 2 * LANES), 1) & (LANES - 1)
    sel = [jnp.where(lane2 < FOX_DH, 1.0, 0.0).astype(BF16),
           jnp.where(lane2 >= FOX_DH, 1.0, 0.0).astype(BF16)]
    row = lax.broadcasted_iota(jnp.int32, (LANES, 1), 0)
    steps = [(qi, kj) for qi in range(nq) for kj in range(qi + 1)]
    stage = (s_a, s_b)

    def logits(i):
        qi, kj = steps[i]
        kb = k_ref[kj * tq:(kj + 1) * tq, :]
        for hh in range(2):
            stage[i % 2][hh] = _dot_nt(kb * sel[hh], q_ref[qi * tq:(qi + 1) * tq, :])

    def softmax_pv(i):
        qi, kj = steps[i]
        diag = kj == qi
        if kj == 0:
            m_s[...] = jnp.full(m_s.shape, NEG_INF, F32)
            l_s[...] = jnp.zeros(l_s.shape, F32)
            acc_s[...] = jnp.zeros(acc_s.shape, F32)
        vt = vt_ref[0, kj]
        for hh in range(2):
            for c0 in range(0, tq, LANES):
                cs = slice(c0, c0 + LANES)
                def load_s(r0):
                    s = stage[i % 2][hh, r0:r0 + LANES, cs]
                    if diag and r0 >= c0:
                        key = lax.broadcasted_iota(jnp.int32, (LANES, LANES), 0) + r0
                        qry = lax.broadcasted_iota(jnp.int32, (LANES, LANES), 1) + c0
                        s = jnp.where(key <= qry, s, NEG_INF)
                    return s

                mx = _fold8(load_s(0), jnp.maximum)
                for r0 in range(LANES, tq, LANES):
                    mx = jnp.maximum(mx, _fold8(load_s(r0), jnp.maximum))
                m_old = m_s[hh, :, cs]
                m_new = jnp.maximum(m_old, jnp.max(mx, axis=0, keepdims=True))
                alpha = jnp.exp2(m_old - m_new)
                sm = None
                for r0 in range(0, tq, LANES):
                    p = jnp.exp2(load_s(r0) - m_new)
                    part = _fold8(p, jnp.add)
                    sm = part if sm is None else sm + part
                    p_s[hh, r0:r0 + LANES, cs] = p.astype(BF16)
                l_s[hh, :, cs] = alpha * l_s[hh, :, cs] + jnp.sum(sm, axis=0, keepdims=True)
                m_s[hh, :, cs] = m_new
                a_s[hh, :, cs] = alpha
            acc_s[hh] = a_s[hh] * acc_s[hh] + _dot(vt, p_s[hh])
        if diag:
            o_t = jnp.where(row < FOX_DH, acc_s[0] / l_s[0], acc_s[1] / l_s[1])
            o_ref[qi * tq:(qi + 1) * tq, :] = o_t.T.astype(BF16)

    for i in range(len(steps)):
        logits(i)
        softmax_pv(i)


def _fox_attn(q2, k2, vbt, *, n_batch, seq, tq):
    t = q2.shape[0]
    nq = seq // tq
    npair = FOX_HEADS // 2
    return pl.pallas_call(
        functools.partial(_fox_attn_body, tq=tq),
        grid=(n_batch, npair),
        in_specs=[pl.BlockSpec((seq, 2 * LANES), lambda b, p: (b, p)),
                  pl.BlockSpec((seq, 2 * LANES), lambda b, p: (b, p)),
                  pl.BlockSpec((1, nq, LANES, tq), lambda b, p: (b, 0, p, 0))],
        out_specs=pl.BlockSpec((seq, LANES), lambda b, p: (b, p)),
        out_shape=jax.ShapeDtypeStruct((t, D_MODEL), BF16),
        scratch_shapes=[pltpu.VMEM((2, tq, tq), F32), pltpu.VMEM((2, tq, tq), F32), pltpu.VMEM((2, tq, tq), BF16),
                        pltpu.VMEM((2, 1, tq), F32), pltpu.VMEM((2, 1, tq), F32), pltpu.VMEM((2, 1, tq), F32),
                        pltpu.VMEM((2, LANES, tq), F32)],
        compiler_params=_params(("arbitrary", "arbitrary")),
        name="fox_attn",
    )(q2, k2, vbt)


def _fox_proj_sample_body(x_ref, g_ref, wq, wk, wv, wf, bf, q_out, k_out, v_out, lf_out):
    h = _rms(x_ref[...], g_ref[...]).astype(BF16)
    q_out[...] = _dot(h, wq[...]) * FOX_SCALE
    k_out[...] = _dot(h, wk[...])
    v_out[...] = _dot(h, wv[...])
    lf_out[...] = _log_sigmoid(_dot(h, wf[...]) + bf[...])[:, :FOX_HEADS]


def _fox_proj_sample(x, g, w):
    t, d = x.shape
    consts = [g, w["wq"], w["wk"], w["wv"], w["wf"], w["bf"]]
    full = lambda n: pl.BlockSpec((t, n), lambda i: (0, 0))
    return pl.pallas_call(
        _fox_proj_sample_body,
        grid=(1,),
        in_specs=[full(d)] + [_const_spec(a.shape) for a in consts],
        out_specs=[full(d), full(d), full(d), full(FOX_HEADS)],
        out_shape=[jax.ShapeDtypeStruct((t, d), F32)] * 3 + [jax.ShapeDtypeStruct((t, FOX_HEADS), F32)],
        compiler_params=_params(("arbitrary",)),
        name="fox_proj_sample",
    )(x, *consts)


def _page_suffix_body(lf_ref, out_ref):
    ji = lax.broadcasted_iota(jnp.int32, (PAGE_SIZE, PAGE_SIZE), 0)
    si = lax.broadcasted_iota(jnp.int32, (PAGE_SIZE, PAGE_SIZE), 1)
    later = jnp.where(ji > si, 1.0, 0.0).astype(BF16)
    ones = jnp.ones((PAGE_SIZE, PAGE_SIZE), BF16)
    hi, mid, lo = _split3(lf_ref[...])
    pages = out_ref.shape[0]
    sfx = _dot(hi, later) + _dot(mid, later) + _dot(lo, later)
    tot = _dot(hi, ones) + _dot(mid, ones) + _dot(lo, ones)
    out_ref[:, 0:FOX_HEADS, :] = sfx.reshape(pages, FOX_HEADS, PAGE_SIZE)
    out_ref[:, FOX_HEADS:2 * FOX_HEADS, :] = tot.reshape(pages, FOX_HEADS, PAGE_SIZE)


def _page_suffix(lf_t, pages_per_step):
    r, p = lf_t.shape
    n_pool = r // FOX_HEADS
    return pl.pallas_call(
        _page_suffix_body,
        grid=(n_pool // pages_per_step,),
        in_specs=[pl.BlockSpec((pages_per_step * FOX_HEADS, p), lambda i: (i, 0))],
        out_specs=pl.BlockSpec((pages_per_step, 2 * FOX_HEADS, p), lambda i: (i, 0, 0)),
        out_shape=jax.ShapeDtypeStruct((n_pool, 2 * FOX_HEADS, p), F32),
        compiler_params=_params(("arbitrary",)),
        name="page_suffix",
    )(lf_t)


def _fox_decode_body(pt_ref, qbd_ref, kn_ref, vn_ref, lfn_ref, *refs, n_q, pages_per_step):
    pp = pages_per_step
    page_refs = refs[:3 * pp]
    o_ref, m_s, l_s, acc_s, run_s, ccol_s = refs[3 * pp:]
    del pt_ref
    j = pl.program_id(1)
    hq = FOX_HEADS * n_q
    qbd = qbd_ref[0]

    def attend(s_parts, vt_parts):
        s = s_parts[0] if len(s_parts) == 1 else jnp.concatenate(s_parts, axis=1)
        m_old = m_s[...]
        m_new = jnp.maximum(m_old, jnp.max(s, axis=1, keepdims=True))
        alpha = jnp.exp(m_old - m_new)
        p = jnp.exp(s - m_new)
        l_s[...] = alpha * l_s[...] + jnp.sum(p, axis=1, keepdims=True)
        pb = p.astype(BF16)
        pv = _dot_nt(pb[:, 0:PAGE_SIZE], vt_parts[0])
        for i in range(1, len(vt_parts)):
            pv = pv + _dot_nt(pb[:, i * PAGE_SIZE:(i + 1) * PAGE_SIZE], vt_parts[i])
        acc_s[...] = alpha * acc_s[...] + pv
        m_s[...] = m_new

    def tile_q(bias16):
        return jnp.concatenate([bias16] * n_q, axis=0)

    @pl.when(j == 0)
    def _():
        lfn = lfn_ref[0]
        cn = lfn
        for sh in range(1, n_q):
            cn = cn + pltpu.roll(lfn, sh, axis=1)
        ccol = jnp.concatenate(
            [jnp.broadcast_to(cn[:, qq:qq + 1], (FOX_HEADS, LANES)) for qq in range(n_q)], axis=0)
        ccol_s[...] = ccol
        m_s[...] = jnp.full(m_s.shape, NEG_INF, F32)
        l_s[...] = jnp.zeros(l_s.shape, F32)
        acc_s[...] = jnp.zeros(acc_s.shape, F32)
        run_s[...] = jnp.zeros(run_s.shape, F32)
        qrow = lax.broadcasted_iota(jnp.int32, (hq, LANES), 0) >> 4
        key = lax.broadcasted_iota(jnp.int32, (hq, LANES), 1)
        s_new = _dot(qbd, kn_ref[0]) + tile_q(-cn) + ccol
        attend([jnp.where(key <= qrow, s_new, NEG_INF)], [vn_ref[0]])

    run = run_s[...]
    ccol = ccol_s[...]
    s_parts, vt_parts = [], []
    for i in range(pp):
        kt_ref, vt_ref, b_ref = page_refs[3 * i:3 * i + 3]
        bias16 = b_ref[0, 0:FOX_HEADS, :] + run
        s_parts.append(_dot(qbd, kt_ref[0].astype(BF16)) + tile_q(bias16) + ccol)
        run = run + b_ref[0, FOX_HEADS:2 * FOX_HEADS, :]
        vt_parts.append(vt_ref[0].astype(BF16))
    run_s[...] = run
    attend(s_parts, vt_parts)

    @pl.when(j == pl.num_programs(1) - 1)
    def _():
        res = acc_s[...] / l_s[...]
        hrow = lax.broadcasted_iota(jnp.int32, (FOX_HEADS, D_MODEL), 0)
        hcol = lax.broadcasted_iota(jnp.int32, (FOX_HEADS, D_MODEL), 1) >> 6
        own = hrow == hcol
        rows = [jnp.sum(jnp.where(own, res[qq * FOX_HEADS:(qq + 1) * FOX_HEADS, :], 0.0), axis=0, keepdims=True)
                for qq in range(n_q)]
        o_ref[0] = jnp.concatenate(rows, axis=0)


def _fox_decode(page_table, qbd, kt_new, vt_new, lf_new_t, kt_pool, vt_pool, bias_pool, *, n_q, pages_per_step):
    n_batch, n_pages = page_table.shape
    pp = pages_per_step
    hq = FOX_HEADS * n_q
    d = D_MODEL
    steps = n_pages // pp
    pt = page_table.reshape(-1)

    def page_map(i):
        def f(b, j, pt_ref):
            return (pt_ref[b * n_pages + (n_pages - 1 - (j * pp + i))], 0, 0)
        return f

    per_batch = lambda shape: pl.BlockSpec((1,) + shape, lambda b, j, pt_ref: (b, 0, 0))
    in_specs = [per_batch((hq, d)), per_batch((d, PAGE_SIZE)), per_batch((d, PAGE_SIZE)),
                per_batch((FOX_HEADS, LANES))]
    args = [qbd, kt_new, vt_new, lf_new_t]
    for i in range(pp):
        in_specs += [pl.BlockSpec((1, d, PAGE_SIZE), page_map(i)), pl.BlockSpec((1, d, PAGE_SIZE), page_map(i)),
                     pl.BlockSpec((1, 2 * FOX_HEADS, PAGE_SIZE), page_map(i))]
        args += [kt_pool, vt_pool, bias_pool]
    grid_spec = pltpu.PrefetchScalarGridSpec(
        num_scalar_prefetch=1,
        grid=(n_batch, steps),
        in_specs=in_specs,
        out_specs=pl.BlockSpec((1, n_q, d), lambda b, j, pt_ref: (b, 0, 0)),
        scratch_shapes=[pltpu.VMEM((hq, 1), F32), pltpu.VMEM((hq, 1), F32), pltpu.VMEM((hq, d), F32),
                        pltpu.VMEM((FOX_HEADS, LANES), F32), pltpu.VMEM((hq, LANES), F32)],
    )
    return pl.pallas_call(
        functools.partial(_fox_decode_body, n_q=n_q, pages_per_step=pp),
        grid_spec=grid_spec,
        out_shape=jax.ShapeDtypeStruct((n_batch, n_q, d), F32),
        compiler_params=_params(("arbitrary", "arbitrary")),
        name="fox_decode",
    )(pt, *args)


def _conv_tail(y, x, g_post, lng, lnb, w2, b2):
    mu = jnp.mean(y, axis=-1, keepdims=True)
    yc = y - mu
    var = jnp.mean(yc * yc, axis=-1, keepdims=True)
    yn = yc * lax.rsqrt(var + LN_EPS) * lng + lnb
    m = _dot(_silu(yn).astype(BF16), w2) + b2
    return x + _rms(m, g_post)


def _glu_in(x, g_pre, w1, b1):
    h = _rms(x, g_pre).astype(BF16)
    u2 = _dot(h, w1) + b1
    return u2[:, :D_MODEL] * jax.nn.sigmoid(u2[:, D_MODEL:])


_CONV_RB = 128


def _conv_prompt_body(x_ref, g_ref, w1, b1, wdw, bdw, lng, lnb, w2, b2, o_ref, st_ref, ext_ref, y_ref):
    tl = x_ref.shape[0]
    l = pl.program_id(1)
    x = x_ref[...]

    @pl.when(l == 0)
    def _():
        ext_ref[0:CONV_HDR, :] = jnp.zeros((CONV_HDR, D_MODEL), F32)

    ext_ref[CONV_HDR:CONV_HDR + tl, :] = _glu_in(x, g_ref[0:1, :], w1[...], b1[...])
    base = CONV_HDR - (CONV_WIDTH - 1)
    sub = 8
    for r0 in range(0, tl, _CONV_RB):
        for c0 in range(0, D_MODEL, LANES):
            cs = slice(c0, c0 + LANES)
            acc = jnp.broadcast_to(bdw[:, cs], (_CONV_RB, LANES))
            for s in range(sub):
                n = _CONV_RB if s == 0 else _CONV_RB + sub
                z = None
                for a in range((base + CONV_WIDTH - 1) // sub + 1):
                    wi = sub * a + s - base
                    if 0 <= wi < CONV_WIDTH:
                        term = ext_ref[r0 + sub * a:r0 + sub * a + n, cs] * wdw[wi:wi + 1, cs]
                        z = term if z is None else z + term
                acc = acc + z[s:s + _CONV_RB]
            y_ref[r0:r0 + _CONV_RB, cs] = acc
    o_ref[...] = _conv_tail(y_ref[...], x, g_ref[1:2, :], lng[...], lnb[...], w2[...], b2[...])

    @pl.when(l == pl.num_programs(1) - 1)
    def _():
        st_ref[0] = ext_ref[CONV_HDR + tl - (CONV_WIDTH - 1):CONV_HDR + tl, :]

    ext_ref[0:CONV_HDR, :] = ext_ref[tl:tl + CONV_HDR, :]


def _conv_prompt(x, g2, w, *, n_batch, seq, tl):
    t, d = x.shape
    nl = seq // tl
    consts = [g2, w["w1"], w["b1"], w["wdw"], w["bdw"], w["lng"], w["lnb"], w["w2"], w["b2"]]
    row = lambda i, j: (i * nl + j, 0)
    return pl.pallas_call(
        _conv_prompt_body,
        grid=(n_batch, nl),
        in_specs=[pl.BlockSpec((tl, d), row)] + [_const_spec(a.shape) for a in consts],
        out_specs=[pl.BlockSpec((tl, d), row),
                   pl.BlockSpec((1, CONV_WIDTH - 1, d), lambda i, j: (i, 0, 0))],
        out_shape=[jax.ShapeDtypeStruct((t, d), F32),
                   jax.ShapeDtypeStruct((n_batch, CONV_WIDTH - 1, d), F32)],
        scratch_shapes=[pltpu.VMEM((CONV_HDR + tl, d), F32), pltpu.VMEM((tl, d), F32)],
        compiler_params=_params(("arbitrary", "arbitrary")),
        name="conv_prompt",
    )(x, *consts)


_SAMPLE_PAD = 8


def _conv_sample_body(x_ref, st_in, g_ref, w1, b1, wdw, bdw, lng, lnb, w2, b2, o_ref, st_out, ext_ref, u_ref, y_ref,
                      *, n_q):
    n_batch = st_in.shape[0]
    x = x_ref[...]
    u_ref[...] = _glu_in(x, g_ref[0:1, :], w1[...], b1[...])
    base = CONV_HDR - (CONV_WIDTH - 1)

    def per_batch(b, carry):
        r0 = pl.multiple_of(b * _SAMPLE_PAD, _SAMPLE_PAD)
        ext_ref[0:CONV_HDR, :] = st_in[b]
        ext_ref[CONV_HDR:CONV_HDR + _SAMPLE_PAD, :] = u_ref[pl.ds(r0, _SAMPLE_PAD), :]
        acc = jnp.broadcast_to(bdw[...], (_SAMPLE_PAD, D_MODEL))
        for wi in range(CONV_WIDTH):
            acc = acc + ext_ref[base + wi:base + wi + _SAMPLE_PAD, :] * wdw[wi:wi + 1, :]
        y_ref[pl.ds(r0, _SAMPLE_PAD), :] = acc
        st_out[b] = ext_ref[base + n_q:base + n_q + CONV_WIDTH - 1, :]
        return carry

    lax.fori_loop(0, n_batch, per_batch, 0)
    o_ref[...] = _conv_tail(y_ref[...], x, g_ref[1:2, :], lng[...], lnb[...], w2[...], b2[...])


def _conv_sample(x_pad, st_pad, g2, w, *, n_q):
    t, d = x_pad.shape
    n_batch = st_pad.shape[0]
    consts = [g2, w["w1"], w["b1"], w["wdw"], w["bdw"], w["lng"], w["lnb"], w["w2"], w["b2"]]
    return pl.pallas_call(
        functools.partial(_conv_sample_body, n_q=n_q),
        grid=(1,),
        in_specs=[pl.BlockSpec((t, d), lambda i: (0, 0)),
                  pl.BlockSpec(st_pad.shape, lambda i: (0, 0, 0))] + [_const_spec(a.shape) for a in consts],
        out_specs=[pl.BlockSpec((t, d), lambda i: (0, 0)),
                   pl.BlockSpec((n_batch, CONV_WIDTH - 1, d), lambda i: (0, 0, 0))],
        out_shape=[jax.ShapeDtypeStruct((t, d), F32),
                   jax.ShapeDtypeStruct((n_batch, CONV_WIDTH - 1, d), F32)],
        scratch_shapes=[pltpu.VMEM((CONV_HDR + _SAMPLE_PAD, d), F32), pltpu.VMEM((t, d), F32),
                        pltpu.VMEM((t, d), F32)],
        compiler_params=_params(("arbitrary",)),
        name="conv_sample",
    )(x_pad, st_pad, *consts)


def _row(v):
    return v.reshape(1, -1).astype(F32)


def _pad_cols(a, n):
    return jnp.pad(a, ((0, 0), (0, n - a.shape[1])))


def _gla_weights(wq, wk, wv, wg1, wg2, bg, wr, gn, wo):
    rank = wg1.shape[1]
    return dict(wq=wq.astype(BF16), wk=wk.astype(BF16), wv=wv.astype(BF16), wr=wr.astype(BF16),
                wg1=_pad_cols(wg1, LANES).astype(BF16),
                wg2=jnp.pad(wg2, ((0, LANES - rank), (0, 0))).astype(BF16),
                bg=_row(bg), gn=_row(gn), wo=wo.astype(BF16))


def _gla_sample(xs, s0, g2, w, *, n_batch, n_q):
    d = xs.shape[1]
    c = GLA_CHUNK
    xpad = jnp.pad(xs.reshape(n_batch, n_q, d), ((0, 0), (0, c - n_q), (0, 0))).reshape(n_batch * c, d)
    out, s_fin = _gla_layer(xpad, s0, g2, w, n_batch=n_batch, seq=c, nb=8, tl=c, valid=n_q)
    return out.reshape(n_batch, c, d)[:, :n_q].reshape(n_batch * n_q, d), s_fin


def _fox_sample(xs, g_pre, w, k_pool, v_pool, lf_pool, page_table, *, n_batch, n_q):
    d = D_MODEL
    q, k_new, v_new, lf_new = _fox_proj_sample(xs, g_pre, w)
    q4 = q.reshape(n_batch, n_q, 1, FOX_HEADS, FOX_DH)
    eye = jnp.eye(FOX_HEADS, dtype=F32)[None, None, :, :, None]
    qbd = (q4 * eye).reshape(n_batch, n_q * FOX_HEADS, d).astype(BF16)
    new_t = lambda a: jnp.pad(a.reshape(n_batch, n_q, d).transpose(0, 2, 1),
                              ((0, 0), (0, 0), (0, PAGE_SIZE - n_q))).astype(BF16)
    lf_new_t = jnp.pad(lf_new.reshape(n_batch, n_q, FOX_HEADS).transpose(0, 2, 1),
                       ((0, 0), (0, 0), (0, LANES - n_q)))
    n_pool = lf_pool.shape[0]
    pool_t = lambda a: a.transpose(0, 2, 3, 1).reshape(n_pool, d, PAGE_SIZE)
    lf_t = lf_pool.transpose(0, 2, 1).reshape(n_pool * FOX_HEADS, PAGE_SIZE)
    bias_pool = _page_suffix(lf_t, pages_per_step=n_pool // 8)
    o = _fox_decode(page_table, qbd, new_t(k_new), new_t(v_new), lf_new_t,
                    pool_t(k_pool), pool_t(v_pool), bias_pool, n_q=n_q,
                    pages_per_step=min(16, page_table.shape[1]))
    return o.reshape(n_batch * n_q, d), k_new, v_new, lf_new


def kernel(x_prompt, x_sample, state_gla, cache_fox_k, cache_fox_v, cache_fox_logf, state_conv, page_table,
           norm_g, gla_wq, gla_wk, gla_wv, gla_wg1, gla_wg2, gla_bg, gla_wr, gla_gn, gla_wo,
           fox_wq, fox_wk, fox_wv, fox_wf, fox_bf, fox_wo,
           conv_w1, conv_b1, conv_wdw, conv_bdw, conv_ln_g, conv_ln_b, conv_w2, conv_b2,
           ffn_w1, ffn_w3, ffn_w2):
    bp, seq, d = x_prompt.shape
    bs, n_q, _ = x_sample.shape
    xp = x_prompt.reshape(bp * seq, d)
    xs = x_sample.reshape(bs * n_q, d)
    tp = 512
    w1_all, w3_all, w2_all = ffn_w1.astype(BF16), ffn_w3.astype(BF16), ffn_w2.astype(BF16)
    gla_p, gla_s, cv_p, cv_s = [], [], [], []
    fox_out = None
    for i in range(DEPTH):
        j = i // N_MIXERS
        kind = i % N_MIXERS
        g_mix = norm_g[i, 0:2].astype(F32)
        mix_p = mix_s = None
        if kind == 0:
            w = _gla_weights(gla_wq[j], gla_wk[j], gla_wv[j], gla_wg1[j], gla_wg2[j], gla_bg[j], gla_wr[j],
                             gla_gn[j], gla_wo[j])
            xp, sp = _gla_layer(xp, None, g_mix, w, n_batch=bp, seq=seq, nb=1, tl=tp, valid=tp)
            xs, ss = _gla_sample(xs, (state_gla, j), g_mix, w, n_batch=bs, n_q=n_q)
            gla_p.append(sp)
            gla_s.append(ss)
        elif kind == 1:
            w = dict(wq=fox_wq[j].astype(BF16), wk=fox_wk[j].astype(BF16), wv=fox_wv[j].astype(BF16),
                     wf=_pad_cols(fox_wf[j], LANES).astype(BF16), bf=_pad_cols(_row(fox_bf[j]), LANES))
            wo = fox_wo[j].astype(BF16)
            zero_b = jnp.zeros((1, d), F32)
            g_pre, g_post = g_mix[0:1], g_mix[1:2]
            ktp, vtp, lfp, q2, k2, vbt = _fox_proj_prompt(xp, g_pre, w, _fox_bias_tables(), n_batch=bp, seq=seq, tl=tp)
            op = _fox_attn(q2, k2, vbt, n_batch=bp, seq=seq, tq=tp)
            os_, kn, vn, lfn = _fox_sample(xs, g_pre, w, cache_fox_k[j], cache_fox_v[j], cache_fox_logf[j],
                                           page_table, n_batch=bs, n_q=n_q)
            mix_p, mix_s = (op, wo, zero_b, g_post), (os_, wo, zero_b, g_post)
            cache_view = lambda a: a.reshape(1, bp, FOX_HEADS, FOX_DH, seq).transpose(0, 1, 4, 2, 3)
            fox_out = (cache_view(ktp), cache_view(vtp),
                       lfp.reshape(1, bp, seq, FOX_HEADS),
                       kn.reshape(1, bs, n_q, FOX_HEADS, FOX_DH), vn.reshape(1, bs, n_q, FOX_HEADS, FOX_DH),
                       lfn.reshape(1, bs, n_q, FOX_HEADS))
        else:
            w = dict(w1=conv_w1[j].astype(BF16), b1=_row(conv_b1[j]), wdw=conv_wdw[j].astype(F32),
                     bdw=_row(conv_bdw[j]), lng=_row(conv_ln_g[j]), lnb=_row(conv_ln_b[j]),
                     w2=conv_w2[j].astype(BF16), b2=_row(conv_b2[j]))
            xp, stp = _conv_prompt(xp, g_mix, w, n_batch=bp, seq=seq, tl=tp)
            xs_pad = jnp.pad(xs.reshape(bs, n_q, d), ((0, 0), (0, _SAMPLE_PAD - n_q), (0, 0)))
            st_pad = jnp.pad(state_conv[j], ((0, 0), (CONV_HDR - (CONV_WIDTH - 1), 0), (0, 0)))
            xs_pad, sts = _conv_sample(xs_pad.reshape(bs * _SAMPLE_PAD, d), st_pad, g_mix, w, n_q=n_q)
            xs = xs_pad.reshape(bs, _SAMPLE_PAD, d)[:, :n_q].reshape(bs * n_q, d)
            cv_p.append(stp)
            cv_s.append(sts)
        g_ffn = norm_g[i, 2:4].astype(F32)
        xp = _ffn(xp, g_ffn, w1_all, w3_all, w2_all, i, tp, mixer=mix_p)
        xs = _ffn(xs, g_ffn, w1_all, w3_all, w2_all, i, bs * n_q, mixer=mix_s)
    return (xp.reshape(bp, seq, d), xs.reshape(bs, n_q, d), jnp.stack(gla_p), jnp.stack(gla_s),
            *fox_out, jnp.stack(cv_p), jnp.stack(cv_s))
```

```python
import functools

import numpy as np
import jax
import jax.numpy as jnp
from jax import lax
from jax.experimental import pallas as pl
from jax.experimental.pallas import tpu as pltpu

F32 = jnp.float32
BF16 = jnp.bfloat16

D_MODEL = 1024
DEPTH = 4
N_MIXERS = 3
GLA_HEADS = 4
GLA_DK = 128
GLA_DV = 256
GLA_TAU = 16.0
GLA_CHUNK = 64
FOX_HEADS = 16
FOX_DH = 64
FOX_SCALE = FOX_DH ** -0.5
LOG2E = 1.4426950408889634
PAGE_SIZE = 128
NEG_INF = -1e30
CONV_WIDTH = 31
CONV_HDR = 32
D_FF = 2816
RMS_EPS = 1e-6
LN_EPS = 1e-5

LANES = 128
VMEM_LIMIT = 52 * 1024 * 1024


def _dot(a, b):
    return jnp.dot(a, b, preferred_element_type=F32)


def _dot_nt(a, b):
    return lax.dot_general(a, b, (((1,), (1,)), ((), ())), preferred_element_type=F32)


def _dot_tn(a, b):
    return lax.dot_general(a, b, (((0,), (0,)), ((), ())), preferred_element_type=F32)


def _rms(x, g):
    return x * lax.rsqrt(jnp.mean(x * x, axis=-1, keepdims=True) + RMS_EPS) * g


def _silu(x):
    return x * jax.nn.sigmoid(x)


def _log_sigmoid(z):
    return jnp.minimum(z, 0.0) - jnp.log1p(jnp.exp(-jnp.abs(z)))


def _split3(a):
    hi = a.astype(BF16)
    r1 = a - hi.astype(F32)
    mid = r1.astype(BF16)
    lo = (r1 - mid.astype(F32)).astype(BF16)
    return hi, mid, lo


def _dot3_left(m, a):
    hi, mid, lo = _split3(a)
    return _dot(m, hi) + _dot(m, mid) + _dot(m, lo)


def _dot3_right(a, m):
    hi, mid, lo = _split3(a)
    return _dot(hi, m) + _dot(mid, m) + _dot(lo, m)


def _const_spec(shape):
    nd = len(shape)
    return pl.BlockSpec(shape, lambda *_: (0,) * nd, pipeline_mode=pl.Buffered(1))


def _params(sem):
    return pltpu.CompilerParams(dimension_semantics=sem, vmem_limit_bytes=VMEM_LIMIT)


_FF_CHUNKS = tuple((s, min(512, D_FF - s)) for s in range(0, D_FF, 512))


def _ffn_body(*refs, mixer_proj):
    if mixer_proj:
        x_ref, mo_ref, wo_ref, bo_ref, go_ref, g_ref, w1_ref, w3_ref, w2_ref, o_ref, acc_ref = refs
        x = x_ref[...] + _rms(_dot(mo_ref[...].astype(BF16), wo_ref[...]) + bo_ref[...], go_ref[...])
    else:
        x_ref, g_ref, w1_ref, w3_ref, w2_ref, o_ref, acc_ref = refs
        x = x_ref[...]
    h = _rms(x, g_ref[0:1, :]).astype(BF16)
    for idx, (s, n) in enumerate(_FF_CHUNKS):
        a = _dot(h, w1_ref[0, :, s:s + n])
        b = _dot(h, w3_ref[0, :, s:s + n])
        u = (_silu(a) * b).astype(BF16)
        y = _dot(u, w2_ref[0, s:s + n, :])
        if idx == 0:
            acc_ref[...] = y
        else:
            acc_ref[...] += y
    o_ref[...] = x + _rms(acc_ref[...], g_ref[1:2, :])


def _ffn(x, g2, w1, w3, w2, layer, tm, mixer=None):
    t, d = x.shape
    tile = lambda n: pl.BlockSpec((tm, n), lambda i: (i, 0))
    args, in_specs = [x], [tile(d)]
    if mixer is not None:
        o, wo, bo, go = mixer
        args += [o, wo, bo, go]
        in_specs += [tile(o.shape[1]), _const_spec(wo.shape), _const_spec(bo.shape), _const_spec(go.shape)]
    consts = [g2, w1, w3, w2]
    layer_spec = lambda a: pl.BlockSpec((1,) + a.shape[1:], lambda i: (layer, 0, 0), pipeline_mode=pl.Buffered(1))
    return pl.pallas_call(
        functools.partial(_ffn_body, mixer_proj=mixer is not None),
        grid=(t // tm,),
        in_specs=in_specs + [_const_spec(g2.shape), layer_spec(w1), layer_spec(w3), layer_spec(w2)],
        out_specs=tile(d),
        out_shape=jax.ShapeDtypeStruct((t, d), F32),
        scratch_shapes=[pltpu.VMEM((tm, d), F32)],
        compiler_params=_params(("arbitrary",)),
        name="ffn",
    )(*args, *consts)


def _gla_body(*refs, tl, valid, has_s0, multi_batch):
    if has_s0:
        x_ref, s0_ref = refs[0], refs[1]
        rest = refs[2:]
    else:
        x_ref, s0_ref = refs[0], None
        rest = refs[1:]
    (g_ref, wq, wk, wv, wr, wg1, wg2, bg, gn, wo,
     o_ref, s_ref, q_s, k_s, lg_s, v_s, gt_s, og_s, cum_s, qe_s, ke_s, kd_s, oi_s, u_s) = rest
    c = GLA_CHUNK
    rows = x_ref.shape[0]
    n_chunks = rows // c
    j = pl.program_id(1)

    x = x_ref[...]
    @pl.when(j == 0)
    def _():
        if has_s0:
            s_ref[...] = s0_ref[0]
        else:
            s_ref[...] = jnp.zeros(s_ref.shape, F32)

    h = _rms(x, g_ref[0:1, :]).astype(BF16)
    q_s[...] = _dot(h, wq[...]) * (GLA_DK ** -0.5)
    k = _dot(h, wk[...])
    z = _dot(_dot(h, wg1[...]).astype(BF16), wg2[...]) + bg[...]
    lg = _log_sigmoid(z) * (1.0 / GLA_TAU)
    if valid < tl:
        assert tl & (tl - 1) == 0
        row = lax.broadcasted_iota(jnp.int32, (rows, 1), 0) & (tl - 1)
        keep = row < valid
        k = jnp.where(keep, k, 0.0)
        lg = jnp.where(keep, lg, 0.0)
    k_s[...] = k
    lg_s[...] = lg
    v_s[...] = _dot(h, wv[...]).astype(BF16)
    gt_s[...] = _silu(_dot(h, wr[...]))

    ri = lax.broadcasted_iota(jnp.int32, (c, c), 0)
    ci = lax.broadcasted_iota(jnp.int32, (c, c), 1)
    causal = ri >= ci
    tri = jnp.where(causal, 1.0, 0.0).astype(BF16)
    eye = (lax.broadcasted_iota(jnp.int32, (GLA_DK, GLA_DK), 0)
           == lax.broadcasted_iota(jnp.int32, (GLA_DK, GLA_DK), 1))

    chunk_rows = [slice(ch * c, (ch + 1) * c) for ch in range(n_chunks)]
    head_k = [slice(hd * GLA_DK, (hd + 1) * GLA_DK) for hd in range(GLA_HEADS)]
    head_v = [slice(hd * GLA_DV, (hd + 1) * GLA_DV) for hd in range(GLA_HEADS)]

    for rs in chunk_rows:
        cum_s[rs, :] = _dot3_left(tri, lg_s[rs, :])
    cum = cum_s[...]
    qe_s[...] = (q_s[...] * jnp.exp(cum)).astype(BF16)
    ke_s[...] = (k_s[...] * jnp.exp(-cum)).astype(BF16)
    for ch, rs in enumerate(chunk_rows):
        last = cum_s[(ch + 1) * c - 1:(ch + 1) * c, :]
        kd_s[rs, :] = (k_s[rs, :] * jnp.exp(last - cum_s[rs, :])).astype(BF16)
    for ch, rs in enumerate(chunk_rows):
        for hd in range(GLA_HEADS):
            ks, vs = head_k[hd], head_v[hd]
            att = jnp.where(causal, _dot_nt(qe_s[rs, ks], ke_s[rs, ks]), 0.0).astype(BF16)
            vc = v_s[rs, vs]
            oi_s[rs, vs] = _dot(att, vc)
            u_s[ch, hd] = _dot_tn(kd_s[rs, ks], vc)

    for ch, rs in enumerate(chunk_rows):
        nb = ch if multi_batch else 0
        for hd in range(GLA_HEADS):
            ks, vs = head_k[hd], head_v[hd]
            s_old = s_ref[nb, hd]
            o = oi_s[rs, vs] + _dot(qe_s[rs, ks], s_old.astype(BF16))
            last = cum_s[(ch + 1) * c - 1:(ch + 1) * c, ks]
            dcol = jnp.sum(jnp.where(eye, jnp.exp(last), 0.0), axis=1, keepdims=True)
            s_ref[nb, hd] = s_old * dcol + u_s[ch, hd]
            on = o * lax.rsqrt(jnp.mean(o * o, axis=-1, keepdims=True) + RMS_EPS) * gn[...]
            og_s[rs, vs] = (on * gt_s[rs, vs]).astype(BF16)
    y = _dot(og_s[...], wo[...])
    o_ref[...] = x + _rms(y, g_ref[1:2, :])


def _gla_layer(x, s0, g2, w, *, n_batch, seq, nb, tl, valid):
    t, d = x.shape
    rows = nb * tl
    nl = seq // tl
    assert nb == 1 or nl == 1
    hk, hv = GLA_HEADS * GLA_DK, GLA_HEADS * GLA_DV
    has_s0 = s0 is not None
    st_spec = pl.BlockSpec((nb, GLA_HEADS, GLA_DK, GLA_DV), lambda i, j: (i, 0, 0, 0))
    in_specs = [pl.BlockSpec((rows, d), lambda i, j: (i * nl + j, 0))]
    args = [x]
    if has_s0:
        s0_all, s0_layer = s0
        in_specs.append(pl.BlockSpec((1, nb, GLA_HEADS, GLA_DK, GLA_DV), lambda i, j: (s0_layer, i, 0, 0, 0)))
        args.append(s0_all)
    weights = [g2, w["wq"], w["wk"], w["wv"], w["wr"], w["wg1"], w["wg2"], w["bg"], w["gn"], w["wo"]]
    in_specs += [_const_spec(a.shape) for a in weights]
    args += weights
    body = functools.partial(_gla_body, tl=tl, valid=valid, has_s0=has_s0, multi_batch=nb > 1)
    return pl.pallas_call(
        body,
        grid=(n_batch // nb, nl),
        in_specs=in_specs,
        out_specs=[pl.BlockSpec((rows, d), lambda i, j: (i * nl + j, 0)), st_spec],
        out_shape=[jax.ShapeDtypeStruct((t, d), F32),
                   jax.ShapeDtypeStruct((n_batch, GLA_HEADS, GLA_DK, GLA_DV), F32)],
        scratch_shapes=[pltpu.VMEM((rows, hk), F32), pltpu.VMEM((rows, hk), F32), pltpu.VMEM((rows, hk), F32),
                        pltpu.VMEM((rows, hv), BF16), pltpu.VMEM((rows, hv), F32), pltpu.VMEM((rows, hv), BF16),
                        pltpu.VMEM((rows, hk), F32), pltpu.VMEM((rows, hk), BF16), pltpu.VMEM((rows, hk), BF16),
                        pltpu.VMEM((rows, hk), BF16), pltpu.VMEM((rows, hv), F32),
                        pltpu.VMEM((rows // GLA_CHUNK, GLA_HEADS, GLA_DK, GLA_DV), F32)],
        compiler_params=_params(("arbitrary", "arbitrary")),
        name="gla_layer",
    )(*args)


def _fox_bias_tables():
    pq = np.zeros((3 * LANES, D_MODEL), np.float32)
    pk = np.zeros((3 * LANES, D_MODEL), np.float32)
    oq = np.zeros((1, D_MODEL), np.float32)
    ok = np.zeros((1, D_MODEL), np.float32)
    for hh in range(FOX_HEADS):
        for i in range(3):
            pq[i * LANES + hh, hh * FOX_DH + i] = 1.0
            ok[0, hh * FOX_DH + i] = 1.0
            oq[0, hh * FOX_DH + 3 + i] = 1.0
            pk[i * LANES + hh, hh * FOX_DH + 3 + i] = -1.0
    return (jnp.asarray(pq, BF16), jnp.asarray(pk, BF16), jnp.asarray(oq, F32), jnp.asarray(ok, F32))


def _fox_proj_prompt_body(x_ref, g_ref, wq, wk, wv, wf, bf, pq, pk, oq, ok, tri_ref,
                          kt_out, vt_out, lf_out, q2_out, k2_out, vbt_out, carry_ref):
    tl = x_ref.shape[0]

    @pl.when(pl.program_id(1) == 0)
    def _():
        carry_ref[...] = jnp.zeros(carry_ref.shape, F32)

    h = _rms(x_ref[...], g_ref[...]).astype(BF16)
    q = (_dot(h, wq[...]) * (FOX_SCALE * LOG2E)).astype(BF16)
    k = _dot(h, wk[...])
    v = _dot(h, wv[...])
    kt_out[0] = k.T
    vt = v.T
    vt_out[0] = vt
    vbt_out[0, 0] = vt.astype(BF16)
    lf = _log_sigmoid(_dot(h, wf[...]) + bf[...])
    lf_out[...] = lf[:, :FOX_HEADS]
    cum = _dot3_left(tri_ref[...], lf) + carry_ref[...]
    carry_ref[...] = cum[tl - 1:tl, :]
    c3 = jnp.concatenate(_split3(cum * LOG2E), axis=1)
    qb = (_dot(c3, pq[...]) + oq[...]).astype(BF16)
    kb = (_dot(c3, pk[...]) + ok[...]).astype(BF16)
    kh = k.astype(BF16)
    for hp in range(FOX_HEADS // 2):
        a, b = hp * LANES, (hp + 1) * LANES
        q2_out[:, 2 * a:2 * a + LANES] = q[:, a:b]
        q2_out[:, 2 * a + LANES:2 * b] = qb[:, a:b]
        k2_out[:, 2 * a:2 * a + LANES] = kh[:, a:b]
        k2_out[:, 2 * a + LANES:2 * b] = kb[:, a:b]


def _fox_proj_prompt(x, g, w, tabs, *, n_batch, seq, tl):
    t, d = x.shape
    nl = seq // tl
    pq, pk, oq, ok = tabs
    tri = jnp.asarray(np.tril(np.ones((tl, tl), np.float32)), BF16)
    consts = [g, w["wq"], w["wk"], w["wv"], w["wf"], w["bf"], pq, pk, oq, ok, tri]
    row = lambda i, j: (i * nl + j, 0)
    return pl.pallas_call(
        _fox_proj_prompt_body,
        grid=(n_batch, nl),
        in_specs=[pl.BlockSpec((tl, d), row)] + [_const_spec(a.shape) for a in consts],
        out_specs=[pl.BlockSpec((1, d, tl), lambda i, j: (i, 0, j)), pl.BlockSpec((1, d, tl), lambda i, j: (i, 0, j)),
                   pl.BlockSpec((tl, FOX_HEADS), row),
                   pl.BlockSpec((tl, 2 * d), row), pl.BlockSpec((tl, 2 * d), row),
                   pl.BlockSpec((1, 1, d, tl), lambda i, j: (i, j, 0, 0))],
        out_shape=[jax.ShapeDtypeStruct((n_batch, d, seq), F32), jax.ShapeDtypeStruct((n_batch, d, seq), F32),
                   jax.ShapeDtypeStruct((t, FOX_HEADS), F32),
                   jax.ShapeDtypeStruct((t, 2 * d), BF16), jax.ShapeDtypeStruct((t, 2 * d), BF16),
                   jax.ShapeDtypeStruct((n_batch, nl, d, tl), BF16)],
        scratch_shapes=[pltpu.VMEM((1, LANES), F32)],
        compiler_params=_params(("arbitrary", "arbitrary")),
        name="fox_proj_prompt",
    )(x, *consts)


def _fold8(x, op):
    n = x.shape[0]
    while n > 8:
        n //= 2
        x = op(x[:n], x[n:])
    return x


def _fox_attn_body(q_ref, k_ref, vt_ref, o_ref, s_a, s_b, p_s, m_s, l_s, a_s, acc_s, *, tq):
    nq = q_ref.shape[0] // tq
    lane2 = lax.broadcasted_iota(jnp.int32, (1, 2 * LANES), 1) & (LANES - 1)
    sel = [jnp.where(lane2 < FOX_DH, 1.0, 0.0).astype(BF16),
           jnp.where(lane2 >= FOX_DH, 1.0, 0.0).astype(BF16)]
    row = lax.broadcasted_iota(jnp.int32, (LANES, 1), 0)
    steps = [(qi, kj) for qi in range(nq) for kj in range(qi + 1)]
    stage = (s_a, s_b)

    def logits(i):
        qi, kj = steps[i]
        kb = k_ref[kj * tq:(kj + 1) * tq, :]
        for hh in range(2):
            stage[i % 2][hh] = _dot_nt(kb * sel[hh], q_ref[qi * tq:(qi + 1) * tq, :])

    def softmax_pv(i):
        qi, kj = steps[i]
        diag = kj == qi
        if kj == 0:
            m_s[...] = jnp.full(m_s.shape, NEG_INF, F32)
            l_s[...] = jnp.zeros(l_s.shape, F32)
            acc_s[...] = jnp.zeros(acc_s.shape, F32)
        vt = vt_ref[0, kj]
        for hh in range(2):
            for c0 in range(0, tq, LANES):
                cs = slice(c0, c0 + LANES)
                def load_s(r0):
                    s = stage[i % 2][hh, r0:r0 + LANES, cs]
                    if diag and r0 >= c0:
                        key = lax.broadcasted_iota(jnp.int32, (LANES, LANES), 0) + r0
                        qry = lax.broadcasted_iota(jnp.int32, (LANES, LANES), 1) + c0
                        s = jnp.where(key <= qry, s, NEG_INF)
                    return s

                mx = _fold8(load_s(0), jnp.maximum)
                for r0 in range(LANES, tq, LANES):
                    mx = jnp.maximum(mx, _fold8(load_s(r0), jnp.maximum))
                m_old = m_s[hh, :, cs]
                m_new = jnp.maximum(m_old, jnp.max(mx, axis=0, keepdims=True))
                alpha = jnp.exp2(m_old - m_new)
                sm = None
                for r0 in range(0, tq, LANES):
                    p = jnp.exp2(load_s(r0) - m_new)
                    part = _fold8(p, jnp.add)
                    sm = part if sm is None else sm + part
                    p_s[hh, r0:r0 + LANES, cs] = p.astype(BF16)
                l_s[hh, :, cs] = alpha * l_s[hh, :, cs] + jnp.sum(sm, axis=0, keepdims=True)
                m_s[hh, :, cs] = m_new
                a_s[hh, :, cs] = alpha
            acc_s[hh] = a_s[hh] * acc_s[hh] + _dot(vt, p_s[hh])
        if diag:
            o_t = jnp.where(row < FOX_DH, acc_s[0] / l_s[0], acc_s[1] / l_s[1])
            o_ref[qi * tq:(qi + 1) * tq, :] = o_t.T.astype(BF16)

    for i in range(len(steps)):
        logits(i)
        softmax_pv(i)


def _fox_attn(q2, k2, vbt, *, n_batch, seq, tq):
    t = q2.shape[0]
    nq = seq // tq
    npair = FOX_HEADS // 2
    return pl.pallas_call(
        functools.partial(_fox_attn_body, tq=tq),
        grid=(n_batch, npair),
        in_specs=[pl.BlockSpec((seq, 2 * LANES), lambda b, p: (b, p)),
                  pl.BlockSpec((seq, 2 * LANES), lambda b, p: (b, p)),
                  pl.BlockSpec((1, nq, LANES, tq), lambda b, p: (b, 0, p, 0))],
        out_specs=pl.BlockSpec((seq, LANES), lambda b, p: (b, p)),
        out_shape=jax.ShapeDtypeStruct((t, D_MODEL), BF16),
        scratch_shapes=[pltpu.VMEM((2, tq, tq), F32), pltpu.VMEM((2, tq, tq), F32), pltpu.VMEM((2, tq, tq), BF16),
                        pltpu.VMEM((2, 1, tq), F32), pltpu.VMEM((2, 1, tq), F32), pltpu.VMEM((2, 1, tq), F32),
                        pltpu.VMEM((2, LANES, tq), F32)],
        compiler_params=_params(("arbitrary", "arbitrary")),
        name="fox_attn",
    )(q2, k2, vbt)


def _fox_proj_sample_body(x_ref, g_ref, wq, wk, wv, wf, bf, q_out, k_out, v_out, lf_out):
    h = _rms(x_ref[...], g_ref[...]).astype(BF16)
    q_out[...] = _dot(h, wq[...]) * FOX_SCALE
    k_out[...] = _dot(h, wk[...])
    v_out[...] = _dot(h, wv[...])
    lf_out[...] = _log_sigmoid(_dot(h, wf[...]) + bf[...])[:, :FOX_HEADS]


def _fox_proj_sample(x, g, w):
    t, d = x.shape
    consts = [g, w["wq"], w["wk"], w["wv"], w["wf"], w["bf"]]
    full = lambda n: pl.BlockSpec((t, n), lambda i: (0, 0))
    return pl.pallas_call(
        _fox_proj_sample_body,
        grid=(1,),
        in_specs=[full(d)] + [_const_spec(a.shape) for a in consts],
        out_specs=[full(d), full(d), full(d), full(FOX_HEADS)],
        out_shape=[jax.ShapeDtypeStruct((t, d), F32)] * 3 + [jax.ShapeDtypeStruct((t, FOX_HEADS), F32)],
        compiler_params=_params(("arbitrary",)),
        name="fox_proj_sample",
    )(x, *consts)


def _page_suffix_body(lf_ref, out_ref):
    ji = lax.broadcasted_iota(jnp.int32, (PAGE_SIZE, PAGE_SIZE), 0)
    si = lax.broadcasted_iota(jnp.int32, (PAGE_SIZE, PAGE_SIZE), 1)
    later = jnp.where(ji > si, 1.0, 0.0).astype(BF16)
    ones = jnp.ones((PAGE_SIZE, PAGE_SIZE), BF16)
    hi, mid, lo = _split3(lf_ref[...])
    pages = out_ref.shape[0]
    sfx = _dot(hi, later) + _dot(mid, later) + _dot(lo, later)
    tot = _dot(hi, ones) + _dot(mid, ones) + _dot(lo, ones)
    out_ref[:, 0:FOX_HEADS, :] = sfx.reshape(pages, FOX_HEADS, PAGE_SIZE)
    out_ref[:, FOX_HEADS:2 * FOX_HEADS, :] = tot.reshape(pages, FOX_HEADS, PAGE_SIZE)


def _page_suffix(lf_t, pages_per_step):
    r, p = lf_t.shape
    n_pool = r // FOX_HEADS
    return pl.pallas_call(
        _page_suffix_body,
        grid=(n_pool // pages_per_step,),
        in_specs=[pl.BlockSpec((pages_per_step * FOX_HEADS, p), lambda i: (i, 0))],
        out_specs=pl.BlockSpec((pages_per_step, 2 * FOX_HEADS, p), lambda i: (i, 0, 0)),
        out_shape=jax.ShapeDtypeStruct((n_pool, 2 * FOX_HEADS, p), F32),
        compiler_params=_params(("arbitrary",)),
        name="page_suffix",
    )(lf_t)


def _fox_decode_body(pt_ref, q_ref, kn_ref, vn_ref, lfn_ref, *refs, n_q, pages_per_step):
    pp = pages_per_step
    page_refs = refs[:3 * pp]
    o_ref, m_s, l_s, acc_s, run_s, ccol_s, qbd_s = refs[3 * pp:]
    del pt_ref
    j = pl.program_id(1)
    hq = FOX_HEADS * n_q
    hrow = lax.broadcasted_iota(jnp.int32, (FOX_HEADS, D_MODEL), 0)
    hcol = lax.broadcasted_iota(jnp.int32, (FOX_HEADS, D_MODEL), 1) >> 6
    own = hrow == hcol

    def attend(s_parts, vt_parts):
        s = s_parts[0] if len(s_parts) == 1 else jnp.concatenate(s_parts, axis=1)
        m_old = m_s[...]
        m_new = jnp.maximum(m_old, jnp.max(s, axis=1, keepdims=True))
        alpha = jnp.exp(m_old - m_new)
        p = jnp.exp(s - m_new)
        l_s[...] = alpha * l_s[...] + jnp.sum(p, axis=1, keepdims=True)
        pb = p.astype(BF16)
        pv = _dot_nt(pb[:, 0:PAGE_SIZE], vt_parts[0])
        for i in range(1, len(vt_parts)):
            pv = pv + _dot_nt(pb[:, i * PAGE_SIZE:(i + 1) * PAGE_SIZE], vt_parts[i])
        acc_s[...] = alpha * acc_s[...] + pv
        m_s[...] = m_new

    def tile_q(bias16):
        return jnp.concatenate([bias16] * n_q, axis=0)

    @pl.when(j == 0)
    def _():
        lfn = lfn_ref[0]
        cn = lfn
        for sh in range(1, n_q):
            cn = cn + pltpu.roll(lfn, sh, axis=1)
        ccol = jnp.concatenate(
            [jnp.broadcast_to(cn[:, qq:qq + 1], (FOX_HEADS, LANES)) for qq in range(n_q)], axis=0)
        ccol_s[...] = ccol
        qbd_s[...] = jnp.concatenate(
            [jnp.where(own, jnp.broadcast_to(q_ref[0, qq:qq + 1, :], (FOX_HEADS, D_MODEL)), 0.0)
             for qq in range(n_q)], axis=0).astype(BF16)
        m_s[...] = jnp.full(m_s.shape, NEG_INF, F32)
        l_s[...] = jnp.zeros(l_s.shape, F32)
        acc_s[...] = jnp.zeros(acc_s.shape, F32)
        run_s[...] = jnp.zeros(run_s.shape, F32)
        qrow = lax.broadcasted_iota(jnp.int32, (hq, LANES), 0) >> 4
        key = lax.broadcasted_iota(jnp.int32, (hq, LANES), 1)
        s_new = _dot(qbd_s[...], kn_ref[0]) + tile_q(-cn) + ccol
        attend([jnp.where(key <= qrow, s_new, NEG_INF)], [vn_ref[0]])

    run = run_s[...]
    ccol = ccol_s[...]
    qbd = qbd_s[...]
    s_parts, vt_parts = [], []
    for i in range(pp):
        kt_ref, vt_ref, b_ref = page_refs[3 * i:3 * i + 3]
        bias16 = b_ref[0, 0:FOX_HEADS, :] + run
        s_parts.append(_dot(qbd, kt_ref[0].astype(BF16)) + tile_q(bias16) + ccol)
        run = run + b_ref[0, FOX_HEADS:2 * FOX_HEADS, :]
        vt_parts.append(vt_ref[0].astype(BF16))
    run_s[...] = run
    attend(s_parts, vt_parts)

    @pl.when(j == pl.num_programs(1) - 1)
    def _():
        res = acc_s[...] / l_s[...]
        rows = [jnp.sum(jnp.where(own, res[qq * FOX_HEADS:(qq + 1) * FOX_HEADS, :], 0.0), axis=0, keepdims=True)
                for qq in range(n_q)]
        o_ref[0] = jnp.concatenate(rows, axis=0)


def _fox_decode(page_table, q, kt_new, vt_new, lf_new_t, kt_pool, vt_pool, bias_pool, *, n_q, pages_per_step):
    n_batch, n_pages = page_table.shape
    pp = pages_per_step
    hq = FOX_HEADS * n_q
    d = D_MODEL
    steps = n_pages // pp
    pt = page_table.reshape(-1)

    def page_map(i):
        def f(b, j, pt_ref):
            return (pt_ref[b * n_pages + (n_pages - 1 - (j * pp + i))], 0, 0)
        return f

    per_batch = lambda shape: pl.BlockSpec((1,) + shape, lambda b, j, pt_ref: (b, 0, 0))
    in_specs = [per_batch((n_q, d)), per_batch((d, PAGE_SIZE)), per_batch((d, PAGE_SIZE)),
                per_batch((FOX_HEADS, LANES))]
    args = [q, kt_new, vt_new, lf_new_t]
    for i in range(pp):
        in_specs += [pl.BlockSpec((1, d, PAGE_SIZE), page_map(i)), pl.BlockSpec((1, d, PAGE_SIZE), page_map(i)),
                     pl.BlockSpec((1, 2 * FOX_HEADS, PAGE_SIZE), page_map(i))]
        args += [kt_pool, vt_pool, bias_pool]
    grid_spec = pltpu.PrefetchScalarGridSpec(
        num_scalar_prefetch=1,
        grid=(n_batch, steps),
        in_specs=in_specs,
        out_specs=pl.BlockSpec((1, n_q, d), lambda b, j, pt_ref: (b, 0, 0)),
        scratch_shapes=[pltpu.VMEM((hq, 1), F32), pltpu.VMEM((hq, 1), F32), pltpu.VMEM((hq, d), F32),
                        pltpu.VMEM((FOX_HEADS, LANES), F32), pltpu.VMEM((hq, LANES), F32),
                        pltpu.VMEM((hq, d), BF16)],
    )
    return pl.pallas_call(
        functools.partial(_fox_decode_body, n_q=n_q, pages_per_step=pp),
        grid_spec=grid_spec,
        out_shape=jax.ShapeDtypeStruct((n_batch, n_q, d), F32),
        compiler_params=_params(("arbitrary", "arbitrary")),
        name="fox_decode",
    )(pt, *args)


def _conv_tail(y, x, g_post, lng, lnb, w2, b2):
    mu = jnp.mean(y, axis=-1, keepdims=True)
    yc = y - mu
    var = jnp.mean(yc * yc, axis=-1, keepdims=True)
    yn = yc * lax.rsqrt(var + LN_EPS) * lng + lnb
    m = _dot(_silu(yn).astype(BF16), w2) + b2
    return x + _rms(m, g_post)


def _glu_in(x, g_pre, w1, b1):
    h = _rms(x, g_pre).astype(BF16)
    u2 = _dot(h, w1) + b1
    return u2[:, :D_MODEL] * jax.nn.sigmoid(u2[:, D_MODEL:])


_CONV_RB = 128


def _conv_prompt_body(x_ref, g_ref, w1, b1, wdw, bdw, lng, lnb, w2, b2, o_ref, st_ref, ext_ref, y_ref):
    tl = x_ref.shape[0]
    l = pl.program_id(1)
    x = x_ref[...]

    @pl.when(l == 0)
    def _():
        ext_ref[0:CONV_HDR, :] = jnp.zeros((CONV_HDR, D_MODEL), F32)

    ext_ref[CONV_HDR:CONV_HDR + tl, :] = _glu_in(x, g_ref[0:1, :], w1[...], b1[...])
    base = CONV_HDR - (CONV_WIDTH - 1)
    sub = 8
    for r0 in range(0, tl, _CONV_RB):
        for c0 in range(0, D_MODEL, LANES):
            cs = slice(c0, c0 + LANES)
            acc = jnp.broadcast_to(bdw[:, cs], (_CONV_RB, LANES))
            for s in range(sub):
                n = _CONV_RB if s == 0 else _CONV_RB + sub
                z = None
                for a in range((base + CONV_WIDTH - 1) // sub + 1):
                    wi = sub * a + s - base
                    if 0 <= wi < CONV_WIDTH:
                        term = ext_ref[r0 + sub * a:r0 + sub * a + n, cs] * wdw[wi:wi + 1, cs]
                        z = term if z is None else z + term
                acc = acc + z[s:s + _CONV_RB]
            y_ref[r0:r0 + _CONV_RB, cs] = acc
    o_ref[...] = _conv_tail(y_ref[...], x, g_ref[1:2, :], lng[...], lnb[...], w2[...], b2[...])

    @pl.when(l == pl.num_programs(1) - 1)
    def _():
        st_ref[0] = ext_ref[CONV_HDR + tl - (CONV_WIDTH - 1):CONV_HDR + tl, :]

    ext_ref[0:CONV_HDR, :] = ext_ref[tl:tl + CONV_HDR, :]


def _conv_prompt(x, g2, w, *, n_batch, seq, tl):
    t, d = x.shape
    nl = seq // tl
    consts = [g2, w["w1"], w["b1"], w["wdw"], w["bdw"], w["lng"], w["lnb"], w["w2"], w["b2"]]
    row = lambda i, j: (i * nl + j, 0)
    return pl.pallas_call(
        _conv_prompt_body,
        grid=(n_batch, nl),
        in_specs=[pl.BlockSpec((tl, d), row)] + [_const_spec(a.shape) for a in consts],
        out_specs=[pl.BlockSpec((tl, d), row),
                   pl.BlockSpec((1, CONV_WIDTH - 1, d), lambda i, j: (i, 0, 0))],
        out_shape=[jax.ShapeDtypeStruct((t, d), F32),
                   jax.ShapeDtypeStruct((n_batch, CONV_WIDTH - 1, d), F32)],
        scratch_shapes=[pltpu.VMEM((CONV_HDR + tl, d), F32), pltpu.VMEM((tl, d), F32)],
        compiler_params=_params(("arbitrary", "arbitrary")),
        name="conv_prompt",
    )(x, *consts)


_SAMPLE_PAD = 8


def _conv_sample_body(x_ref, st_in, g_ref, w1, b1, wdw, bdw, lng, lnb, w2, b2, o_ref, st_out, ext_ref, u_ref, y_ref,
                      *, n_q):
    n_batch = st_in.shape[0]
    x = x_ref[...]
    u_ref[...] = _glu_in(x, g_ref[0:1, :], w1[...], b1[...])
    base = CONV_HDR - (CONV_WIDTH - 1)

    def per_batch(b, carry):
        r0 = pl.multiple_of(b * _SAMPLE_PAD, _SAMPLE_PAD)
        ext_ref[0:CONV_HDR, :] = st_in[b]
        ext_ref[CONV_HDR:CONV_HDR + _SAMPLE_PAD, :] = u_ref[pl.ds(r0, _SAMPLE_PAD), :]
        acc = jnp.broadcast_to(bdw[...], (_SAMPLE_PAD, D_MODEL))
        for wi in range(CONV_WIDTH):
            acc = acc + ext_ref[base + wi:base + wi + _SAMPLE_PAD, :] * wdw[wi:wi + 1, :]
        y_ref[pl.ds(r0, _SAMPLE_PAD), :] = acc
        st_out[b] = ext_ref[base + n_q:base + n_q + CONV_WIDTH - 1, :]
        return carry

    lax.fori_loop(0, n_batch, per_batch, 0)
    o_ref[...] = _conv_tail(y_ref[...], x, g_ref[1:2, :], lng[...], lnb[...], w2[...], b2[...])


def _conv_sample(x_pad, st_pad, g2, w, *, n_q):
    t, d = x_pad.shape
    n_batch = st_pad.shape[0]
    consts = [g2, w["w1"], w["b1"], w["wdw"], w["bdw"], w["lng"], w["lnb"], w["w2"], w["b2"]]
    return pl.pallas_call(
        functools.partial(_conv_sample_body, n_q=n_q),
        grid=(1,),
        in_specs=[pl.BlockSpec((t, d), lambda i: (0, 0)),
                  pl.BlockSpec(st_pad.shape, lambda i: (0, 0, 0))] + [_const_spec(a.shape) for a in consts],
        out_specs=[pl.BlockSpec((t, d), lambda i: (0, 0)),
                   pl.BlockSpec((n_batch, CONV_WIDTH - 1, d), lambda i: (0, 0, 0))],
        out_shape=[jax.ShapeDtypeStruct((t, d), F32),
                   jax.ShapeDtypeStruct((n_batch, CONV_WIDTH - 1, d), F32)],
        scratch_shapes=[pltpu.VMEM((CONV_HDR + _SAMPLE_PAD, d), F32), pltpu.VMEM((t, d), F32),
                        pltpu.VMEM((t, d), F32)],
        compiler_params=_params(("arbitrary",)),
        name="conv_sample",
    )(x_pad, st_pad, *consts)


def _row(v):
    return v.reshape(1, -1).astype(F32)


def _pad_cols(a, n):
    return jnp.pad(a, ((0, 0), (0, n - a.shape[1])))


def _gla_weights(wq, wk, wv, wg1, wg2, bg, wr, gn, wo):
    rank = wg1.shape[1]
    return dict(wq=wq.astype(BF16), wk=wk.astype(BF16), wv=wv.astype(BF16), wr=wr.astype(BF16),
                wg1=_pad_cols(wg1, LANES).astype(BF16),
                wg2=jnp.pad(wg2, ((0, LANES - rank), (0, 0))).astype(BF16),
                bg=_row(bg), gn=_row(gn), wo=wo.astype(BF16))


def _gla_sample(xs, s0, g2, w, *, n_batch, n_q):
    d = xs.shape[1]
    c = GLA_CHUNK
    xpad = jnp.pad(xs.reshape(n_batch, n_q, d), ((0, 0), (0, c - n_q), (0, 0))).reshape(n_batch * c, d)
    out, s_fin = _gla_layer(xpad, s0, g2, w, n_batch=n_batch, seq=c, nb=8, tl=c, valid=n_q)
    return out.reshape(n_batch, c, d)[:, :n_q].reshape(n_batch * n_q, d), s_fin


def _fox_sample(xs, g_pre, w, k_pool, v_pool, lf_pool, page_table, *, n_batch, n_q):
    d = D_MODEL
    q, k_new, v_new, lf_new = _fox_proj_sample(xs, g_pre, w)
    new_t = lambda a: jnp.pad(a.reshape(n_batch, n_q, d).transpose(0, 2, 1),
                              ((0, 0), (0, 0), (0, PAGE_SIZE - n_q))).astype(BF16)
    lf_new_t = jnp.pad(lf_new.reshape(n_batch, n_q, FOX_HEADS).transpose(0, 2, 1),
                       ((0, 0), (0, 0), (0, LANES - n_q)))
    n_pool = lf_pool.shape[0]
    pool_t = lambda a: a.transpose(0, 2, 3, 1).reshape(n_pool, d, PAGE_SIZE)
    lf_t = lf_pool.transpose(0, 2, 1).reshape(n_pool * FOX_HEADS, PAGE_SIZE)
    bias_pool = _page_suffix(lf_t, pages_per_step=n_pool // 8)
    o = _fox_decode(page_table, q.reshape(n_batch, n_q, d), new_t(k_new), new_t(v_new), lf_new_t,
                    pool_t(k_pool), pool_t(v_pool), bias_pool, n_q=n_q,
                    pages_per_step=min(16, page_table.shape[1]))
    return o.reshape(n_batch * n_q, d), k_new, v_new, lf_new


def kernel(x_prompt, x_sample, state_gla, cache_fox_k, cache_fox_v, cache_fox_logf, state_conv, page_table,
           norm_g, gla_wq, gla_wk, gla_wv, gla_wg1, gla_wg2, gla_bg, gla_wr, gla_gn, gla_wo,
           fox_wq, fox_wk, fox_wv, fox_wf, fox_bf, fox_wo,
           conv_w1, conv_b1, conv_wdw, conv_bdw, conv_ln_g, conv_ln_b, conv_w2, conv_b2,
           ffn_w1, ffn_w3, ffn_w2):
    bp, seq, d = x_prompt.shape
    bs, n_q, _ = x_sample.shape
    xp = x_prompt.reshape(bp * seq, d)
    xs = x_sample.reshape(bs * n_q, d)
    tp = 512
    w1_all, w3_all, w2_all = ffn_w1.astype(BF16), ffn_w3.astype(BF16), ffn_w2.astype(BF16)
    gla_p, gla_s, cv_p, cv_s = [], [], [], []
    fox_out = None
    for i in range(DEPTH):
        j = i // N_MIXERS
        kind = i % N_MIXERS
        g_mix = norm_g[i, 0:2].astype(F32)
        mix_p = mix_s = None
        if kind == 0:
            w = _gla_weights(gla_wq[j], gla_wk[j], gla_wv[j], gla_wg1[j], gla_wg2[j], gla_bg[j], gla_wr[j],
                             gla_gn[j], gla_wo[j])
            xp, sp = _gla_layer(xp, None, g_mix, w, n_batch=bp, seq=seq, nb=1, tl=tp, valid=tp)
            xs, ss = _gla_sample(xs, (state_gla, j), g_mix, w, n_batch=bs, n_q=n_q)
            gla_p.append(sp)
            gla_s.append(ss)
        elif kind == 1:
            w = dict(wq=fox_wq[j].astype(BF16), wk=fox_wk[j].astype(BF16), wv=fox_wv[j].astype(BF16),
                     wf=_pad_cols(fox_wf[j], LANES).astype(BF16), bf=_pad_cols(_row(fox_bf[j]), LANES))
            wo = fox_wo[j].astype(BF16)
            zero_b = jnp.zeros((1, d), F32)
            g_pre, g_post = g_mix[0:1], g_mix[1:2]
            ktp, vtp, lfp, q2, k2, vbt = _fox_proj_prompt(xp, g_pre, w, _fox_bias_tables(), n_batch=bp, seq=seq, tl=tp)
            op = _fox_attn(q2, k2, vbt, n_batch=bp, seq=seq, tq=tp)
            os_, kn, vn, lfn = _fox_sample(xs, g_pre, w, cache_fox_k[j], cache_fox_v[j], cache_fox_logf[j],
                                           page_table, n_batch=bs, n_q=n_q)
            mix_p, mix_s = (op, wo, zero_b, g_post), (os_, wo, zero_b, g_post)
            cache_view = lambda a: a.reshape(1, bp, FOX_HEADS, FOX_DH, seq).transpose(0, 1, 4, 2, 3)
            fox_out = (cache_view(ktp), cache_view(vtp),
                       lfp.reshape(1, bp, seq, FOX_HEADS),
                       kn.reshape(1, bs, n_q, FOX_HEADS, FOX_DH), vn.reshape(1, bs, n_q, FOX_HEADS, FOX_DH),
                       lfn.reshape(1, bs, n_q, FOX_HEADS))
        else:
            w = dict(w1=conv_w1[j].astype(BF16), b1=_row(conv_b1[j]), wdw=conv_wdw[j].astype(F32),
                     bdw=_row(conv_bdw[j]), lng=_row(conv_ln_g[j]), lnb=_row(conv_ln_b[j]),
                     w2=conv_w2[j].astype(BF16), b2=_row(conv_b2[j]))
            xp, stp = _conv_prompt(xp, g_mix, w, n_batch=bp, seq=seq, tl=tp)
            xs_pad = jnp.pad(xs.reshape(bs, n_q, d), ((0, 0), (0, _SAMPLE_PAD - n_q), (0, 0)))
            st_pad = jnp.pad(state_conv[j], ((0, 0), (CONV_HDR - (CONV_WIDTH - 1), 0), (0, 0)))
            xs_pad, sts = _conv_sample(xs_pad.reshape(bs * _SAMPLE_PAD, d), st_pad, g_mix, w, n_q=n_q)
            xs = xs_pad.reshape(bs, _SAMPLE_PAD, d)[:, :n_q].reshape(bs * n_q, d)
            cv_p.append(stp)
            cv_s.append(sts)
        g_ffn = norm_g[i, 2:4].astype(F32)
        xp = _ffn(xp, g_ffn, w1_all, w3_all, w2_all, i, tp, mixer=mix_p)
        xs = _ffn(xs, g_ffn, w1_all, w3_all, w2_all, i, bs * n_q, mixer=mix_s)
    return (xp.reshape(bp, seq, d), xs.reshape(bs, n_q, d), jnp.stack(gla_p), jnp.stack(gla_s),
            *fox_out, jnp.stack(cv_p), jnp.stack(cv_s))
```

```python
import functools

import numpy as np
import jax
import jax.numpy as jnp
from jax import lax
from jax.experimental import pallas as pl
from jax.experimental.pallas import tpu as pltpu

F32 = jnp.float32
BF16 = jnp.bfloat16

D_MODEL = 1024
DEPTH = 4
N_MIXERS = 3
GLA_HEADS = 4
GLA_DK = 128
GLA_DV = 256
GLA_TAU = 16.0
GLA_CHUNK = 64
FOX_HEADS = 16
FOX_DH = 64
FOX_SCALE = FOX_DH ** -0.5
LOG2E = 1.4426950408889634
PAGE_SIZE = 128
NEG_INF = -1e30
CONV_WIDTH = 31
CONV_HDR = 32
D_FF = 2816
RMS_EPS = 1e-6
LN_EPS = 1e-5

LANES = 128
VMEM_LIMIT = 52 * 1024 * 1024


def _dot(a, b):
    return jnp.dot(a, b, preferred_element_type=F32)


def _dot_nt(a, b):
    return lax.dot_general(a, b, (((1,), (1,)), ((), ())), preferred_element_type=F32)


def _dot_tn(a, b):
    return lax.dot_general(a, b, (((0,), (0,)), ((), ())), preferred_element_type=F32)


def _rms(x, g):
    return x * lax.rsqrt(jnp.mean(x * x, axis=-1, keepdims=True) + RMS_EPS) * g


def _silu(x):
    return x * jax.nn.sigmoid(x)


def _log_sigmoid(z):
    return jnp.minimum(z, 0.0) - jnp.log1p(jnp.exp(-jnp.abs(z)))


def _split3(a):
    hi = a.astype(BF16)
    r1 = a - hi.astype(F32)
    mid = r1.astype(BF16)
    lo = (r1 - mid.astype(F32)).astype(BF16)
    return hi, mid, lo


def _dot3_left(m, a):
    hi, mid, lo = _split3(a)
    return _dot(m, hi) + _dot(m, mid) + _dot(m, lo)


def _dot3_right(a, m):
    hi, mid, lo = _split3(a)
    return _dot(hi, m) + _dot(mid, m) + _dot(lo, m)


def _const_spec(shape):
    nd = len(shape)
    return pl.BlockSpec(shape, lambda *_: (0,) * nd, pipeline_mode=pl.Buffered(1))


def _params(sem):
    return pltpu.CompilerParams(dimension_semantics=sem, vmem_limit_bytes=VMEM_LIMIT)


_FF_CHUNKS = tuple((s, min(512, D_FF - s)) for s in range(0, D_FF, 512))


def _ffn_body(*refs, mixer_proj):
    if mixer_proj:
        x_ref, mo_ref, wo_ref, bo_ref, go_ref, g_ref, w1_ref, w3_ref, w2_ref, o_ref, acc_ref = refs
        x = x_ref[...] + _rms(_dot(mo_ref[...].astype(BF16), wo_ref[...]) + bo_ref[...], go_ref[...])
    else:
        x_ref, g_ref, w1_ref, w3_ref, w2_ref, o_ref, acc_ref = refs
        x = x_ref[...]
    h = _rms(x, g_ref[0:1, :]).astype(BF16)
    for idx, (s, n) in enumerate(_FF_CHUNKS):
        a = _dot(h, w1_ref[0, :, s:s + n])
        b = _dot(h, w3_ref[0, :, s:s + n])
        u = (_silu(a) * b).astype(BF16)
        y = _dot(u, w2_ref[0, s:s + n, :])
        if idx == 0:
            acc_ref[...] = y
        else:
            acc_ref[...] += y
    o_ref[...] = x + _rms(acc_ref[...], g_ref[1:2, :])


def _ffn(x, g2, w1, w3, w2, layer, tm, mixer=None):
    t, d = x.shape
    tile = lambda n: pl.BlockSpec((tm, n), lambda i: (i, 0))
    args, in_specs = [x], [tile(d)]
    if mixer is not None:
        o, wo, bo, go = mixer
        args += [o, wo, bo, go]
        in_specs += [tile(o.shape[1]), _const_spec(wo.shape), _const_spec(bo.shape), _const_spec(go.shape)]
    consts = [g2, w1, w3, w2]
    layer_spec = lambda a: pl.BlockSpec((1,) + a.shape[1:], lambda i: (layer, 0, 0), pipeline_mode=pl.Buffered(1))
    return pl.pallas_call(
        functools.partial(_ffn_body, mixer_proj=mixer is not None),
        grid=(t // tm,),
        in_specs=in_specs + [_const_spec(g2.shape), layer_spec(w1), layer_spec(w3), layer_spec(w2)],
        out_specs=tile(d),
        out_shape=jax.ShapeDtypeStruct((t, d), F32),
        scratch_shapes=[pltpu.VMEM((tm, d), F32)],
        compiler_params=_params(("arbitrary",)),
        name="ffn",
    )(*args, *consts)


def _gla_body(*refs, tl, valid, has_s0, multi_batch):
    if has_s0:
        x_ref, s0_ref = refs[0], refs[1]
        rest = refs[2:]
    else:
        x_ref, s0_ref = refs[0], None
        rest = refs[1:]
    (g_ref, wq, wk, wv, wr, wg1, wg2, bg, gn, wo,
     o_ref, s_ref, q_s, k_s, lg_s, v_s, gt_s, og_s, cum_s, qe_s, ke_s, kd_s, oi_s, u_s) = rest[:24]
    c = GLA_CHUNK
    rows = q_s.shape[0]
    pr = x_ref.shape[0] * tl // rows
    n_chunks = rows // c
    j = pl.program_id(1)

    x = x_ref[...]
    @pl.when(j == 0)
    def _():
        if has_s0:
            s_ref[...] = s0_ref[0]
        else:
            s_ref[...] = jnp.zeros(s_ref.shape, F32)

    h = _rms(x, g_ref[0:1, :]).astype(BF16)
    q = _dot(h, wq[...]) * (GLA_DK ** -0.5)
    k = _dot(h, wk[...])
    z = _dot(_dot(h, wg1[...]).astype(BF16), wg2[...]) + bg[...]
    lg = _log_sigmoid(z) * (1.0 / GLA_TAU)
    if valid < pr:
        assert pr & (pr - 1) == 0
        row = lax.broadcasted_iota(jnp.int32, (x.shape[0], 1), 0) & (pr - 1)
        keep = row < valid
        k = jnp.where(keep, k, 0.0)
        lg = jnp.where(keep, lg, 0.0)
    v = _dot(h, wv[...])
    gt = _silu(_dot(h, wr[...]))
    if pr == tl:
        q_s[...] = q
        k_s[...] = k
        lg_s[...] = lg
        v_s[...] = v.astype(BF16)
        gt_s[...] = gt
    else:
        vf_s = rest[24]
        slots = (q_s, k_s, lg_s, gt_s, vf_s)

        @pl.when((pl.program_id(0) == 0) & (j == 0))
        def _():
            for ref in slots:
                ref[...] = jnp.zeros(ref.shape, F32)

        for b in range(rows // tl):
            for ref, val in zip(slots, (q, k, lg, gt, v)):
                ref[b * tl:b * tl + pr, :] = val[b * pr:(b + 1) * pr, :]
        v_s[...] = vf_s[...].astype(BF16)

    ri = lax.broadcasted_iota(jnp.int32, (c, c), 0)
    ci = lax.broadcasted_iota(jnp.int32, (c, c), 1)
    causal = ri >= ci
    tri = jnp.where(causal, 1.0, 0.0).astype(BF16)
    eye = (lax.broadcasted_iota(jnp.int32, (GLA_DK, GLA_DK), 0)
           == lax.broadcasted_iota(jnp.int32, (GLA_DK, GLA_DK), 1))

    chunk_rows = [slice(ch * c, (ch + 1) * c) for ch in range(n_chunks)]
    head_k = [slice(hd * GLA_DK, (hd + 1) * GLA_DK) for hd in range(GLA_HEADS)]
    head_v = [slice(hd * GLA_DV, (hd + 1) * GLA_DV) for hd in range(GLA_HEADS)]

    for rs in chunk_rows:
        cum_s[rs, :] = _dot3_left(tri, lg_s[rs, :])
    cum = cum_s[...]
    qe_s[...] = (q_s[...] * jnp.exp(cum)).astype(BF16)
    ke_s[...] = (k_s[...] * jnp.exp(-cum)).astype(BF16)
    for ch, rs in enumerate(chunk_rows):
        last = cum_s[(ch + 1) * c - 1:(ch + 1) * c, :]
        kd_s[rs, :] = (k_s[rs, :] * jnp.exp(last - cum_s[rs, :])).astype(BF16)
    for ch, rs in enumerate(chunk_rows):
        for hd in range(GLA_HEADS):
            ks, vs = head_k[hd], head_v[hd]
            att = jnp.where(causal, _dot_nt(qe_s[rs, ks], ke_s[rs, ks]), 0.0).astype(BF16)
            vc = v_s[rs, vs]
            oi_s[rs, vs] = _dot(att, vc)
            u_s[ch, hd] = _dot_tn(kd_s[rs, ks], vc)

    for ch, rs in enumerate(chunk_rows):
        nb = ch if multi_batch else 0
        for hd in range(GLA_HEADS):
            ks, vs = head_k[hd], head_v[hd]
            s_old = s_ref[nb, hd]
            o = oi_s[rs, vs] + _dot(qe_s[rs, ks], s_old.astype(BF16))
            last = cum_s[(ch + 1) * c - 1:(ch + 1) * c, ks]
            dcol = jnp.sum(jnp.where(eye, jnp.exp(last), 0.0), axis=1, keepdims=True)
            s_ref[nb, hd] = s_old * dcol + u_s[ch, hd]
            on = o * lax.rsqrt(jnp.mean(o * o, axis=-1, keepdims=True) + RMS_EPS) * gn[...]
            og_s[rs, vs] = (on * gt_s[rs, vs]).astype(BF16)
    y = _dot(og_s[...], wo[...])
    if pr == tl:
        o_ref[...] = x + _rms(y, g_ref[1:2, :])
    else:
        for b in range(rows // tl):
            o_ref[b * pr:(b + 1) * pr, :] = (x[b * pr:(b + 1) * pr, :]
                                             + _rms(y[b * tl:b * tl + pr, :], g_ref[1:2, :]))


def _gla_layer(x, s0, g2, w, *, n_batch, seq, nb, tl, valid, pr=None):
    t, d = x.shape
    rows = nb * tl
    nl = seq // tl
    assert nb == 1 or nl == 1
    hk, hv = GLA_HEADS * GLA_DK, GLA_HEADS * GLA_DV
    has_s0 = s0 is not None
    st_spec = pl.BlockSpec((nb, GLA_HEADS, GLA_DK, GLA_DV), lambda i, j: (i, 0, 0, 0))
    pr = tl if pr is None else pr
    assert pr == tl or nl == 1
    x_spec = pl.BlockSpec((nb * pr, d), lambda i, j: (i * nl + j, 0))
    in_specs = [x_spec]
    args = [x]
    if has_s0:
        s0_all, s0_layer = s0
        in_specs.append(pl.BlockSpec((1, nb, GLA_HEADS, GLA_DK, GLA_DV), lambda i, j: (s0_layer, i, 0, 0, 0)))
        args.append(s0_all)
    weights = [g2, w["wq"], w["wk"], w["wv"], w["wr"], w["wg1"], w["wg2"], w["bg"], w["gn"], w["wo"]]
    in_specs += [_const_spec(a.shape) for a in weights]
    args += weights
    body = functools.partial(_gla_body, tl=tl, valid=valid, has_s0=has_s0, multi_batch=nb > 1)
    return pl.pallas_call(
        body,
        grid=(n_batch // nb, nl),
        in_specs=in_specs,
        out_specs=[x_spec, st_spec],
        out_shape=[jax.ShapeDtypeStruct((t, d), F32),
                   jax.ShapeDtypeStruct((n_batch, GLA_HEADS, GLA_DK, GLA_DV), F32)],
        scratch_shapes=[pltpu.VMEM((rows, hk), F32), pltpu.VMEM((rows, hk), F32), pltpu.VMEM((rows, hk), F32),
                        pltpu.VMEM((rows, hv), BF16), pltpu.VMEM((rows, hv), F32), pltpu.VMEM((rows, hv), BF16),
                        pltpu.VMEM((rows, hk), F32), pltpu.VMEM((rows, hk), BF16), pltpu.VMEM((rows, hk), BF16),
                        pltpu.VMEM((rows, hk), BF16), pltpu.VMEM((rows, hv), F32),
                        pltpu.VMEM((rows // GLA_CHUNK, GLA_HEADS, GLA_DK, GLA_DV), F32)]
        + ([pltpu.VMEM((rows, hv), F32)] if pr < tl else []),
        compiler_params=_params(("arbitrary", "arbitrary")),
        name="gla_layer",
    )(*args)


def _fox_bias_tables():
    pq = np.zeros((3 * LANES, D_MODEL), np.float32)
    pk = np.zeros((3 * LANES, D_MODEL), np.float32)
    oq = np.zeros((1, D_MODEL), np.float32)
    ok = np.zeros((1, D_MODEL), np.float32)
    for hh in range(FOX_HEADS):
        for i in range(3):
            pq[i * LANES + hh, hh * FOX_DH + i] = 1.0
            ok[0, hh * FOX_DH + i] = 1.0
            oq[0, hh * FOX_DH + 3 + i] = 1.0
            pk[i * LANES + hh, hh * FOX_DH + 3 + i] = -1.0
    return (jnp.asarray(pq, BF16), jnp.asarray(pk, BF16), jnp.asarray(oq, F32), jnp.asarray(ok, F32))


def _fox_proj_prompt_body(x_ref, g_ref, wq, wk, wv, wf, bf, pq, pk, oq, ok, tri_ref,
                          kt_out, vt_out, lf_out, q2_out, k2_out, vbt_out, carry_ref):
    tl = x_ref.shape[0]

    @pl.when(pl.program_id(1) == 0)
    def _():
        carry_ref[...] = jnp.zeros(carry_ref.shape, F32)

    h = _rms(x_ref[...], g_ref[...]).astype(BF16)
    q = (_dot(h, wq[...]) * (FOX_SCALE * LOG2E)).astype(BF16)
    k = _dot(h, wk[...])
    v = _dot(h, wv[...])
    kt_out[0] = k.T
    vt = v.T
    vt_out[0] = vt
    vbt_out[0, 0] = vt.astype(BF16)
    lf = _log_sigmoid(_dot(h, wf[...]) + bf[...])
    lf_out[...] = lf[:, :FOX_HEADS]
    cum = _dot3_left(tri_ref[...], lf) + carry_ref[...]
    carry_ref[...] = cum[tl - 1:tl, :]
    c3 = jnp.concatenate(_split3(cum * LOG2E), axis=1)
    qb = (_dot(c3, pq[...]) + oq[...]).astype(BF16)
    kb = (_dot(c3, pk[...]) + ok[...]).astype(BF16)
    kh = k.astype(BF16)
    for hp in range(FOX_HEADS // 2):
        a, b = hp * LANES, (hp + 1) * LANES
        q2_out[:, 2 * a:2 * a + LANES] = q[:, a:b]
        q2_out[:, 2 * a + LANES:2 * b] = qb[:, a:b]
        k2_out[:, 2 * a:2 * a + LANES] = kh[:, a:b]
        k2_out[:, 2 * a + LANES:2 * b] = kb[:, a:b]


def _fox_proj_prompt(x, g, w, tabs, *, n_batch, seq, tl):
    t, d = x.shape
    nl = seq // tl
    pq, pk, oq, ok = tabs
    tri = jnp.asarray(np.tril(np.ones((tl, tl), np.float32)), BF16)
    consts = [g, w["wq"], w["wk"], w["wv"], w["wf"], w["bf"], pq, pk, oq, ok, tri]
    row = lambda i, j: (i * nl + j, 0)
    return pl.pallas_call(
        _fox_proj_prompt_body,
        grid=(n_batch, nl),
        in_specs=[pl.BlockSpec((tl, d), row)] + [_const_spec(a.shape) for a in consts],
        out_specs=[pl.BlockSpec((1, d, tl), lambda i, j: (i, 0, j)), pl.BlockSpec((1, d, tl), lambda i, j: (i, 0, j)),
                   pl.BlockSpec((tl, FOX_HEADS), row),
                   pl.BlockSpec((tl, 2 * d), row), pl.BlockSpec((tl, 2 * d), row),
                   pl.BlockSpec((1, 1, d, tl), lambda i, j: (i, j, 0, 0))],
        out_shape=[jax.ShapeDtypeStruct((n_batch, d, seq), F32), jax.ShapeDtypeStruct((n_batch, d, seq), F32),
                   jax.ShapeDtypeStruct((t, FOX_HEADS), F32),
                   jax.ShapeDtypeStruct((t, 2 * d), BF16), jax.ShapeDtypeStruct((t, 2 * d), BF16),
                   jax.ShapeDtypeStruct((n_batch, nl, d, tl), BF16)],
        scratch_shapes=[pltpu.VMEM((1, LANES), F32)],
        compiler_params=_params(("arbitrary", "arbitrary")),
        name="fox_proj_prompt",
    )(x, *consts)


def _fold8(x, op):
    n = x.shape[0]
    while n > 8:
        n //= 2
        x = op(x[:n], x[n:])
    return x


def _fox_attn_body(q_ref, k_ref, vt_ref, o_ref, s_s, p_s, m_s, l_s, a_s, acc_s, *, tq):
    nq = q_ref.shape[0] // tq
    lane2 = lax.broadcasted_iota(jnp.int32, (1, 2 * LANES), 1) & (LANES - 1)
    sel = [jnp.where(lane2 < FOX_DH, 1.0, 0.0).astype(BF16),
           jnp.where(lane2 >= FOX_DH, 1.0, 0.0).astype(BF16)]
    row = lax.broadcasted_iota(jnp.int32, (LANES, 1), 0)
    steps = [(qi, kj) for qi in range(nq) for kj in range(qi + 1)]

    def logits(i):
        qi, kj = steps[i]
        kb = k_ref[kj * tq:(kj + 1) * tq, :]
        for hh in range(2):
            s_s[hh] = _dot_nt(kb * sel[hh], q_ref[qi * tq:(qi + 1) * tq, :])

    def softmax_pv(i):
        qi, kj = steps[i]
        diag = kj == qi
        if kj == 0:
            m_s[...] = jnp.full(m_s.shape, NEG_INF, F32)
            l_s[...] = jnp.zeros(l_s.shape, F32)
            acc_s[...] = jnp.zeros(acc_s.shape, F32)
        vt = vt_ref[0, kj]
        for hh in range(2):
            for c0 in range(0, tq, LANES):
                cs = slice(c0, c0 + LANES)
                def load_s(r0):
                    s = s_s[hh, r0:r0 + LANES, cs]
                    if diag and r0 >= c0:
                        key = lax.broadcasted_iota(jnp.int32, (LANES, LANES), 0) + r0
                        qry = lax.broadcasted_iota(jnp.int32, (LANES, LANES), 1) + c0
                        s = jnp.where(key <= qry, s, NEG_INF)
                    return s

                mx = _fold8(load_s(0), jnp.maximum)
                for r0 in range(LANES, tq, LANES):
                    mx = jnp.maximum(mx, _fold8(load_s(r0), jnp.maximum))
                m_old = m_s[hh, :, cs]
                m_new = jnp.maximum(m_old, jnp.max(mx, axis=0, keepdims=True))
                alpha = jnp.exp2(m_old - m_new)
                sm = None
                for r0 in range(0, tq, LANES):
                    p = jnp.exp2(load_s(r0) - m_new)
                    part = _fold8(p, jnp.add)
                    sm = part if sm is None else sm + part
                    p_s[hh, r0:r0 + LANES, cs] = p.astype(BF16)
                l_s[hh, :, cs] = alpha * l_s[hh, :, cs] + jnp.sum(sm, axis=0, keepdims=True)
                m_s[hh, :, cs] = m_new
                a_s[hh, :, cs] = alpha
            acc_s[hh] = a_s[hh] * acc_s[hh] + _dot(vt, p_s[hh])
        if diag:
            o_t = jnp.where(row < FOX_DH, acc_s[0] / l_s[0], acc_s[1] / l_s[1])
            o_ref[qi * tq:(qi + 1) * tq, :] = o_t.T.astype(BF16)

    for i in range(len(steps)):
        logits(i)
        softmax_pv(i)


def _fox_attn(q2, k2, vbt, *, n_batch, seq, tq):
    t = q2.shape[0]
    nq = seq // tq
    npair = FOX_HEADS // 2
    return pl.pallas_call(
        functools.partial(_fox_attn_body, tq=tq),
        grid=(n_batch, npair),
        in_specs=[pl.BlockSpec((seq, 2 * LANES), lambda b, p: (b, p)),
                  pl.BlockSpec((seq, 2 * LANES), lambda b, p: (b, p)),
                  pl.BlockSpec((1, nq, LANES, tq), lambda b, p: (b, 0, p, 0))],
        out_specs=pl.BlockSpec((seq, LANES), lambda b, p: (b, p)),
        out_shape=jax.ShapeDtypeStruct((t, D_MODEL), BF16),
        scratch_shapes=[pltpu.VMEM((2, tq, tq), F32), pltpu.VMEM((2, tq, tq), BF16),
                        pltpu.VMEM((2, 1, tq), F32), pltpu.VMEM((2, 1, tq), F32), pltpu.VMEM((2, 1, tq), F32),
                        pltpu.VMEM((2, LANES, tq), F32)],
        compiler_params=_params(("arbitrary", "arbitrary")),
        name="fox_attn",
    )(q2, k2, vbt)


def _fox_proj_sample_body(x_ref, g_ref, wq, wk, wv, wf, bf, q_out, k_out, v_out, lf_out):
    h = _rms(x_ref[...], g_ref[...]).astype(BF16)
    q_out[...] = _dot(h, wq[...]) * FOX_SCALE
    k_out[...] = _dot(h, wk[...])
    v_out[...] = _dot(h, wv[...])
    lf_out[...] = _log_sigmoid(_dot(h, wf[...]) + bf[...])[:, :FOX_HEADS]


def _fox_proj_sample(x, g, w):
    t, d = x.shape
    consts = [g, w["wq"], w["wk"], w["wv"], w["wf"], w["bf"]]
    full = lambda n: pl.BlockSpec((t, n), lambda i: (0, 0))
    return pl.pallas_call(
        _fox_proj_sample_body,
        grid=(1,),
        in_specs=[full(d)] + [_const_spec(a.shape) for a in consts],
        out_specs=[full(d), full(d), full(d), full(FOX_HEADS)],
        out_shape=[jax.ShapeDtypeStruct((t, d), F32)] * 3 + [jax.ShapeDtypeStruct((t, FOX_HEADS), F32)],
        compiler_params=_params(("arbitrary",)),
        name="fox_proj_sample",
    )(x, *consts)


def _page_suffix_body(lf_ref, out_ref):
    ji = lax.broadcasted_iota(jnp.int32, (PAGE_SIZE, PAGE_SIZE), 0)
    si = lax.broadcasted_iota(jnp.int32, (PAGE_SIZE, PAGE_SIZE), 1)
    later = jnp.where(ji > si, 1.0, 0.0).astype(BF16)
    ones = jnp.ones((PAGE_SIZE, PAGE_SIZE), BF16)
    hi, mid, lo = _split3(lf_ref[...])
    pages = out_ref.shape[0]
    sfx = _dot(hi, later) + _dot(mid, later) + _dot(lo, later)
    tot = _dot(hi, ones) + _dot(mid, ones) + _dot(lo, ones)
    out_ref[:, 0:FOX_HEADS, :] = sfx.reshape(pages, FOX_HEADS, PAGE_SIZE)
    out_ref[:, FOX_HEADS:2 * FOX_HEADS, :] = tot.reshape(pages, FOX_HEADS, PAGE_SIZE)


def _page_suffix(lf_t, pages_per_step):
    r, p = lf_t.shape
    n_pool = r // FOX_HEADS
    return pl.pallas_call(
        _page_suffix_body,
        grid=(n_pool // pages_per_step,),
        in_specs=[pl.BlockSpec((pages_per_step * FOX_HEADS, p), lambda i: (i, 0))],
        out_specs=pl.BlockSpec((pages_per_step, 2 * FOX_HEADS, p), lambda i: (i, 0, 0)),
        out_shape=jax.ShapeDtypeStruct((n_pool, 2 * FOX_HEADS, p), F32),
        compiler_params=_params(("arbitrary",)),
        name="page_suffix",
    )(lf_t)


def _fox_decode_body(pt_ref, q_ref, kn_ref, vn_ref, lfn_ref, *refs, n_q, pages_per_step):
    pp = pages_per_step
    page_refs = refs[:3 * pp]
    o_ref, m_s, l_s, acc_s, run_s, ccol_s, qbd_s = refs[3 * pp:]
    del pt_ref
    j = pl.program_id(1)
    hq = FOX_HEADS * n_q
    hrow = lax.broadcasted_iota(jnp.int32, (FOX_HEADS, D_MODEL), 0)
    hcol = lax.broadcasted_iota(jnp.int32, (FOX_HEADS, D_MODEL), 1) >> 6
    own = hrow == hcol

    def attend(s_parts, vt_parts):
        s = s_parts[0] if len(s_parts) == 1 else jnp.concatenate(s_parts, axis=1)
        m_old = m_s[...]
        m_new = jnp.maximum(m_old, jnp.max(s, axis=1, keepdims=True))
        alpha = jnp.exp(m_old - m_new)
        p = jnp.exp(s - m_new)
        l_s[...] = alpha * l_s[...] + jnp.sum(p, axis=1, keepdims=True)
        pb = p.astype(BF16)
        pv = _dot_nt(pb[:, 0:PAGE_SIZE], vt_parts[0])
        for i in range(1, len(vt_parts)):
            pv = pv + _dot_nt(pb[:, i * PAGE_SIZE:(i + 1) * PAGE_SIZE], vt_parts[i])
        acc_s[...] = alpha * acc_s[...] + pv
        m_s[...] = m_new

    def tile_q(bias16):
        return jnp.concatenate([bias16] * n_q, axis=0)

    @pl.when(j == 0)
    def _():
        lfn = lfn_ref[0]
        cn = lfn
        for sh in range(1, n_q):
            cn = cn + pltpu.roll(lfn, sh, axis=1)
        ccol = jnp.concatenate(
            [jnp.broadcast_to(cn[:, qq:qq + 1], (FOX_HEADS, LANES)) for qq in range(n_q)], axis=0)
        ccol_s[...] = ccol
        qbd_s[...] = jnp.concatenate(
            [jnp.where(own, jnp.broadcast_to(q_ref[0, qq:qq + 1, :], (FOX_HEADS, D_MODEL)), 0.0)
             for qq in range(n_q)], axis=0).astype(BF16)
        m_s[...] = jnp.full(m_s.shape, NEG_INF, F32)
        l_s[...] = jnp.zeros(l_s.shape, F32)
        acc_s[...] = jnp.zeros(acc_s.shape, F32)
        run_s[...] = jnp.zeros(run_s.shape, F32)
        qrow = lax.broadcasted_iota(jnp.int32, (hq, LANES), 0) >> 4
        key = lax.broadcasted_iota(jnp.int32, (hq, LANES), 1)
        s_new = _dot(qbd_s[...], kn_ref[0]) + tile_q(-cn) + ccol
        attend([jnp.where(key <= qrow, s_new, NEG_INF)], [vn_ref[0]])

    run = run_s[...]
    ccol = ccol_s[...]
    qbd = qbd_s[...]
    s_parts, vt_parts = [], []
    for i in range(pp):
        kt_ref, vt_ref, b_ref = page_refs[3 * i:3 * i + 3]
        bias16 = b_ref[0, 0:FOX_HEADS, :] + run
        s_parts.append(_dot(qbd, kt_ref[0].astype(BF16)) + tile_q(bias16) + ccol)
        run = run + b_ref[0, FOX_HEADS:2 * FOX_HEADS, :]
        vt_parts.append(vt_ref[0].astype(BF16))
    run_s[...] = run
    attend(s_parts, vt_parts)

    @pl.when(j == pl.num_programs(1) - 1)
    def _():
        res = acc_s[...] / l_s[...]
        rows = [jnp.sum(jnp.where(own, res[qq * FOX_HEADS:(qq + 1) * FOX_HEADS, :], 0.0), axis=0, keepdims=True)
                for qq in range(n_q)]
        o_ref[0] = jnp.concatenate(rows, axis=0)


def _fox_decode(page_table, q, kt_new, vt_new, lf_new_t, kt_pool, vt_pool, bias_pool, *, n_q, pages_per_step):
    n_batch, n_pages = page_table.shape
    pp = pages_per_step
    hq = FOX_HEADS * n_q
    d = D_MODEL
    steps = n_pages // pp
    pt = page_table.reshape(-1)

    def page_map(i):
        def f(b, j, pt_ref):
            return (pt_ref[b * n_pages + (n_pages - 1 - (j * pp + i))], 0, 0)
        return f

    per_batch = lambda shape: pl.BlockSpec((1,) + shape, lambda b, j, pt_ref: (b, 0, 0))
    in_specs = [per_batch((n_q, d)), per_batch((d, PAGE_SIZE)), per_batch((d, PAGE_SIZE)),
                per_batch((FOX_HEADS, LANES))]
    args = [q, kt_new, vt_new, lf_new_t]
    for i in range(pp):
        in_specs += [pl.BlockSpec((1, d, PAGE_SIZE), page_map(i)), pl.BlockSpec((1, d, PAGE_SIZE), page_map(i)),
                     pl.BlockSpec((1, 2 * FOX_HEADS, PAGE_SIZE), page_map(i))]
        args += [kt_pool, vt_pool, bias_pool]
    grid_spec = pltpu.PrefetchScalarGridSpec(
        num_scalar_prefetch=1,
        grid=(n_batch, steps),
        in_specs=in_specs,
        out_specs=pl.BlockSpec((1, n_q, d), lambda b, j, pt_ref: (b, 0, 0)),
        scratch_shapes=[pltpu.VMEM((hq, 1), F32), pltpu.VMEM((hq, 1), F32), pltpu.VMEM((hq, d), F32),
                        pltpu.VMEM((FOX_HEADS, LANES), F32), pltpu.VMEM((hq, LANES), F32),
                        pltpu.VMEM((hq, d), BF16)],
    )
    return pl.pallas_call(
        functools.partial(_fox_decode_body, n_q=n_q, pages_per_step=pp),
        grid_spec=grid_spec,
        out_shape=jax.ShapeDtypeStruct((n_batch, n_q, d), F32),
        compiler_params=_params(("arbitrary", "arbitrary")),
        name="fox_decode",
    )(pt, *args)


def _conv_tail(y, x, g_post, lng, lnb, w2, b2):
    mu = jnp.mean(y, axis=-1, keepdims=True)
    yc = y - mu
    var = jnp.mean(yc * yc, axis=-1, keepdims=True)
    yn = yc * lax.rsqrt(var + LN_EPS) * lng + lnb
    m = _dot(_silu(yn).astype(BF16), w2) + b2
    return x + _rms(m, g_post)


def _glu_in(x, g_pre, w1, b1):
    h = _rms(x, g_pre).astype(BF16)
    u2 = _dot(h, w1) + b1
    return u2[:, :D_MODEL] * jax.nn.sigmoid(u2[:, D_MODEL:])


_CONV_RB = 128


def _conv_prompt_body(x_ref, g_ref, w1, b1, wdw, bdw, lng, lnb, w2, b2, o_ref, st_ref, ext_ref, y_ref):
    tl = x_ref.shape[0]
    l = pl.program_id(1)
    x = x_ref[...]

    @pl.when(l == 0)
    def _():
        ext_ref[0:CONV_HDR, :] = jnp.zeros((CONV_HDR, D_MODEL), F32)

    ext_ref[CONV_HDR:CONV_HDR + tl, :] = _glu_in(x, g_ref[0:1, :], w1[...], b1[...])
    base = CONV_HDR - (CONV_WIDTH - 1)
    sub = 8
    for r0 in range(0, tl, _CONV_RB):
        for c0 in range(0, D_MODEL, LANES):
            cs = slice(c0, c0 + LANES)
            acc = jnp.broadcast_to(bdw[:, cs], (_CONV_RB, LANES))
            for s in range(sub):
                n = _CONV_RB if s == 0 else _CONV_RB + sub
                z = None
                for a in range((base + CONV_WIDTH - 1) // sub + 1):
                    wi = sub * a + s - base
                    if 0 <= wi < CONV_WIDTH:
                        term = ext_ref[r0 + sub * a:r0 + sub * a + n, cs] * wdw[wi:wi + 1, cs]
                        z = term if z is None else z + term
                acc = acc + z[s:s + _CONV_RB]
            y_ref[r0:r0 + _CONV_RB, cs] = acc
    o_ref[...] = _conv_tail(y_ref[...], x, g_ref[1:2, :], lng[...], lnb[...], w2[...], b2[...])

    @pl.when(l == pl.num_programs(1) - 1)
    def _():
        st_ref[0] = ext_ref[CONV_HDR + tl - (CONV_WIDTH - 1):CONV_HDR + tl, :]

    ext_ref[0:CONV_HDR, :] = ext_ref[tl:tl + CONV_HDR, :]


def _conv_prompt(x, g2, w, *, n_batch, seq, tl):
    t, d = x.shape
    nl = seq // tl
    consts = [g2, w["w1"], w["b1"], w["wdw"], w["bdw"], w["lng"], w["lnb"], w["w2"], w["b2"]]
    row = lambda i, j: (i * nl + j, 0)
    return pl.pallas_call(
        _conv_prompt_body,
        grid=(n_batch, nl),
        in_specs=[pl.BlockSpec((tl, d), row)] + [_const_spec(a.shape) for a in consts],
        out_specs=[pl.BlockSpec((tl, d), row),
                   pl.BlockSpec((1, CONV_WIDTH - 1, d), lambda i, j: (i, 0, 0))],
        out_shape=[jax.ShapeDtypeStruct((t, d), F32),
                   jax.ShapeDtypeStruct((n_batch, CONV_WIDTH - 1, d), F32)],
        scratch_shapes=[pltpu.VMEM((CONV_HDR + tl, d), F32), pltpu.VMEM((tl, d), F32)],
        compiler_params=_params(("arbitrary", "arbitrary")),
        name="conv_prompt",
    )(x, *consts)


_SAMPLE_PAD = 8


def _conv_sample_body(x_ref, st_in, g_ref, w1, b1, wdw, bdw, lng, lnb, w2, b2, o_ref, st_out, ext_ref, u_ref, y_ref,
                      *, n_q):
    n_batch = st_in.shape[0]
    x = x_ref[...]
    u_ref[...] = _glu_in(x, g_ref[0:1, :], w1[...], b1[...])
    base = CONV_HDR - (CONV_WIDTH - 1)

    def per_batch(b, carry):
        r0 = pl.multiple_of(b * _SAMPLE_PAD, _SAMPLE_PAD)
        ext_ref[0:CONV_HDR, :] = st_in[b]
        ext_ref[CONV_HDR:CONV_HDR + _SAMPLE_PAD, :] = u_ref[pl.ds(r0, _SAMPLE_PAD), :]
        acc = jnp.broadcast_to(bdw[...], (_SAMPLE_PAD, D_MODEL))
        for wi in range(CONV_WIDTH):
            acc = acc + ext_ref[base + wi:base + wi + _SAMPLE_PAD, :] * wdw[wi:wi + 1, :]
        y_ref[pl.ds(r0, _SAMPLE_PAD), :] = acc
        st_out[b] = ext_ref[base + n_q:base + n_q + CONV_WIDTH - 1, :]
        return carry

    lax.fori_loop(0, n_batch, per_batch, 0)
    o_ref[...] = _conv_tail(y_ref[...], x, g_ref[1:2, :], lng[...], lnb[...], w2[...], b2[...])


def _conv_sample(x_pad, st_pad, g2, w, *, n_q):
    t, d = x_pad.shape
    n_batch = st_pad.shape[0]
    consts = [g2, w["w1"], w["b1"], w["wdw"], w["bdw"], w["lng"], w["lnb"], w["w2"], w["b2"]]
    return pl.pallas_call(
        functools.partial(_conv_sample_body, n_q=n_q),
        grid=(1,),
        in_specs=[pl.BlockSpec((t, d), lambda i: (0, 0)),
                  pl.BlockSpec(st_pad.shape, lambda i: (0, 0, 0))] + [_const_spec(a.shape) for a in consts],
        out_specs=[pl.BlockSpec((t, d), lambda i: (0, 0)),
                   pl.BlockSpec((n_batch, CONV_WIDTH - 1, d), lambda i: (0, 0, 0))],
        out_shape=[jax.ShapeDtypeStruct((t, d), F32),
                   jax.ShapeDtypeStruct((n_batch, CONV_WIDTH - 1, d), F32)],
        scratch_shapes=[pltpu.VMEM((CONV_HDR + _SAMPLE_PAD, d), F32), pltpu.VMEM((t, d), F32),
                        pltpu.VMEM((t, d), F32)],
        compiler_params=_params(("arbitrary",)),
        name="conv_sample",
    )(x_pad, st_pad, *consts)


def _row(v):
    return v.reshape(1, -1).astype(F32)


def _pad_cols(a, n):
    return jnp.pad(a, ((0, 0), (0, n - a.shape[1])))


def _gla_weights(wq, wk, wv, wg1, wg2, bg, wr, gn, wo):
    rank = wg1.shape[1]
    return dict(wq=wq.astype(BF16), wk=wk.astype(BF16), wv=wv.astype(BF16), wr=wr.astype(BF16),
                wg1=_pad_cols(wg1, LANES).astype(BF16),
                wg2=jnp.pad(wg2, ((0, LANES - rank), (0, 0))).astype(BF16),
                bg=_row(bg), gn=_row(gn), wo=wo.astype(BF16))


def _gla_sample(xs, s0, g2, w, *, n_batch, n_q):
    d = xs.shape[1]
    c = GLA_CHUNK
    pr = _SAMPLE_PAD
    xpad = jnp.pad(xs.reshape(n_batch, n_q, d), ((0, 0), (0, pr - n_q), (0, 0))).reshape(n_batch * pr, d)
    out, s_fin = _gla_layer(xpad, s0, g2, w, n_batch=n_batch, seq=c, nb=8, tl=c, valid=n_q, pr=pr)
    return out.reshape(n_batch, pr, d)[:, :n_q].reshape(n_batch * n_q, d), s_fin


def _fox_sample(xs, g_pre, w, k_pool, v_pool, lf_pool, page_table, *, n_batch, n_q):
    d = D_MODEL
    q, k_new, v_new, lf_new = _fox_proj_sample(xs, g_pre, w)
    new_t = lambda a: jnp.pad(a.reshape(n_batch, n_q, d).transpose(0, 2, 1),
                              ((0, 0), (0, 0), (0, PAGE_SIZE - n_q))).astype(BF16)
    lf_new_t = jnp.pad(lf_new.reshape(n_batch, n_q, FOX_HEADS).transpose(0, 2, 1),
                       ((0, 0), (0, 0), (0, LANES - n_q)))
    n_pool = lf_pool.shape[0]
    pool_t = lambda a: a.transpose(0, 2, 3, 1).reshape(n_pool, d, PAGE_SIZE)
    lf_t = lf_pool.transpose(0, 2, 1).reshape(n_pool * FOX_HEADS, PAGE_SIZE)
    bias_pool = _page_suffix(lf_t, pages_per_step=n_pool // 8)
    o = _fox_decode(page_table, q.reshape(n_batch, n_q, d), new_t(k_new), new_t(v_new), lf_new_t,
                    pool_t(k_pool), pool_t(v_pool), bias_pool, n_q=n_q,
                    pages_per_step=min(16, page_table.shape[1]))
    return o.reshape(n_batch * n_q, d), k_new, v_new, lf_new


def kernel(x_prompt, x_sample, state_gla, cache_fox_k, cache_fox_v, cache_fox_logf, state_conv, page_table,
           norm_g, gla_wq, gla_wk, gla_wv, gla_wg1, gla_wg2, gla_bg, gla_wr, gla_gn, gla_wo,
           fox_wq, fox_wk, fox_wv, fox_wf, fox_bf, fox_wo,
           conv_w1, conv_b1, conv_wdw, conv_bdw, conv_ln_g, conv_ln_b, conv_w2, conv_b2,
           ffn_w1, ffn_w3, ffn_w2):
    bp, seq, d = x_prompt.shape
    bs, n_q, _ = x_sample.shape
    xp = x_prompt.reshape(bp * seq, d)
    xs = x_sample.reshape(bs * n_q, d)
    tp = 512
    w1_all, w3_all, w2_all = ffn_w1.astype(BF16), ffn_w3.astype(BF16), ffn_w2.astype(BF16)
    gla_p, gla_s, cv_p, cv_s = [], [], [], []
    fox_out = None
    for i in range(DEPTH):
        j = i // N_MIXERS
        kind = i % N_MIXERS
        g_mix = norm_g[i, 0:2].astype(F32)
        mix_p = mix_s = None
        if kind == 0:
            w = _gla_weights(gla_wq[j], gla_wk[j], gla_wv[j], gla_wg1[j], gla_wg2[j], gla_bg[j], gla_wr[j],
                             gla_gn[j], gla_wo[j])
            xp, sp = _gla_layer(xp, None, g_mix, w, n_batch=bp, seq=seq, nb=1, tl=tp, valid=tp)
            xs, ss = _gla_sample(xs, (state_gla, j), g_mix, w, n_batch=bs, n_q=n_q)
            gla_p.append(sp)
            gla_s.append(ss)
        elif kind == 1:
            w = dict(wq=fox_wq[j].astype(BF16), wk=fox_wk[j].astype(BF16), wv=fox_wv[j].astype(BF16),
                     wf=_pad_cols(fox_wf[j], LANES).astype(BF16), bf=_pad_cols(_row(fox_bf[j]), LANES))
            wo = fox_wo[j].astype(BF16)
            zero_b = jnp.zeros((1, d), F32)
            g_pre, g_post = g_mix[0:1], g_mix[1:2]
            ktp, vtp, lfp, q2, k2, vbt = _fox_proj_prompt(xp, g_pre, w, _fox_bias_tables(), n_batch=bp, seq=seq, tl=tp)
            op = _fox_attn(q2, k2, vbt, n_batch=bp, seq=seq, tq=tp)
            os_, kn, vn, lfn = _fox_sample(xs, g_pre, w, cache_fox_k[j], cache_fox_v[j], cache_fox_logf[j],
                                           page_table, n_batch=bs, n_q=n_q)
            mix_p, mix_s = (op, wo, zero_b, g_post), (os_, wo, zero_b, g_post)
            cache_view = lambda a: a.reshape(1, bp, FOX_HEADS, FOX_DH, seq).transpose(0, 1, 4, 2, 3)
            fox_out = (cache_view(ktp), cache_view(vtp),
                       lfp.reshape(1, bp, seq, FOX_HEADS),
                       kn.reshape(1, bs, n_q, FOX_HEADS, FOX_DH), vn.reshape(1, bs, n_q, FOX_HEADS, FOX_DH),
                       lfn.reshape(1, bs, n_q, FOX_HEADS))
        else:
            w = dict(w1=conv_w1[j].astype(BF16), b1=_row(conv_b1[j]), wdw=conv_wdw[j].astype(F32),
                     bdw=_row(conv_bdw[j]), lng=_row(conv_ln_g[j]), lnb=_row(conv_ln_b[j]),
                     w2=conv_w2[j].astype(BF16), b2=_row(conv_b2[j]))
            xp, stp = _conv_prompt(xp, g_mix, w, n_batch=bp, seq=seq, tl=tp)
            xs_pad = jnp.pad(xs.reshape(bs, n_q, d), ((0, 0), (0, _SAMPLE_PAD - n_q), (0, 0)))
            st_pad = jnp.pad(state_conv[j], ((0, 0), (CONV_HDR - (CONV_WIDTH - 1), 0), (0, 0)))
            xs_pad, sts = _conv_sample(xs_pad.reshape(bs * _SAMPLE_PAD, d), st_pad, g_mix, w, n_q=n_q)
            xs = xs_pad.reshape(bs, _SAMPLE_PAD, d)[:, :n_q].reshape(bs * n_q, d)
            cv_p.append(stp)
            cv_s.append(sts)
        g_ffn = norm_g[i, 2:4].astype(F32)
        xp = _ffn(xp, g_ffn, w1_all, w3_all, w2_all, i, tp, mixer=mix_p)
        xs = _ffn(xs, g_ffn, w1_all, w3_all, w2_all, i, bs * n_q, mixer=mix_s)
    return (xp.reshape(bp, seq, d), xs.reshape(bs, n_q, d), jnp.stack(gla_p), jnp.stack(gla_s),
            *fox_out, jnp.stack(cv_p), jnp.stack(cv_s))
```

```python
import functools

import numpy as np
import jax
import jax.numpy as jnp
from jax import lax
from jax.experimental import pallas as pl
from jax.experimental.pallas import tpu as pltpu

F32 = jnp.float32
BF16 = jnp.bfloat16

D_MODEL = 1024
DEPTH = 4
N_MIXERS = 3
GLA_HEADS = 4
GLA_DK = 128
GLA_DV = 256
GLA_TAU = 16.0
GLA_CHUNK = 64
FOX_HEADS = 16
FOX_DH = 64
FOX_SCALE = FOX_DH ** -0.5
LOG2E = 1.4426950408889634
PAGE_SIZE = 128
NEG_INF = -1e30
CONV_WIDTH = 31
CONV_HDR = 32
D_FF = 2816
RMS_EPS = 1e-6
LN_EPS = 1e-5

LANES = 128
VMEM_LIMIT = 52 * 1024 * 1024


def _dot(a, b):
    return jnp.dot(a, b, preferred_element_type=F32)


def _dot_nt(a, b):
    return lax.dot_general(a, b, (((1,), (1,)), ((), ())), preferred_element_type=F32)


def _dot_tn(a, b):
    return lax.dot_general(a, b, (((0,), (0,)), ((), ())), preferred_element_type=F32)


def _rms(x, g):
    return x * lax.rsqrt(jnp.mean(x * x, axis=-1, keepdims=True) + RMS_EPS) * g


def _silu(x):
    return x * jax.nn.sigmoid(x)


def _log_sigmoid(z):
    return jnp.minimum(z, 0.0) - jnp.log1p(jnp.exp(-jnp.abs(z)))


def _split3(a):
    hi = a.astype(BF16)
    r1 = a - hi.astype(F32)
    mid = r1.astype(BF16)
    lo = (r1 - mid.astype(F32)).astype(BF16)
    return hi, mid, lo


def _dot3_left(m, a):
    hi, mid, lo = _split3(a)
    return _dot(m, hi) + _dot(m, mid) + _dot(m, lo)


def _dot3_right(a, m):
    hi, mid, lo = _split3(a)
    return _dot(hi, m) + _dot(mid, m) + _dot(lo, m)


def _const_spec(shape):
    nd = len(shape)
    return pl.BlockSpec(shape, lambda *_: (0,) * nd, pipeline_mode=pl.Buffered(1))


def _params(sem):
    return pltpu.CompilerParams(dimension_semantics=sem, vmem_limit_bytes=VMEM_LIMIT)


_FF_CHUNKS = tuple((s, min(512, D_FF - s)) for s in range(0, D_FF, 512))


def _ffn_body(*refs, mixer_proj):
    if mixer_proj:
        x_ref, mo_ref, wo_ref, bo_ref, go_ref, g_ref, w1_ref, w3_ref, w2_ref, o_ref, acc_ref = refs
        x = x_ref[...] + _rms(_dot(mo_ref[...].astype(BF16), wo_ref[...]) + bo_ref[...], go_ref[...])
    else:
        x_ref, g_ref, w1_ref, w3_ref, w2_ref, o_ref, acc_ref = refs
        x = x_ref[...]
    h = _rms(x, g_ref[0:1, :]).astype(BF16)
    for idx, (s, n) in enumerate(_FF_CHUNKS):
        a = _dot(h, w1_ref[0, :, s:s + n])
        b = _dot(h, w3_ref[0, :, s:s + n])
        u = (_silu(a) * b).astype(BF16)
        y = _dot(u, w2_ref[0, s:s + n, :])
        if idx == 0:
            acc_ref[...] = y
        else:
            acc_ref[...] += y
    o_ref[...] = x + _rms(acc_ref[...], g_ref[1:2, :])


def _ffn(x, g2, w1, w3, w2, layer, tm, mixer=None):
    t, d = x.shape
    tile = lambda n: pl.BlockSpec((tm, n), lambda i: (i, 0))
    args, in_specs = [x], [tile(d)]
    if mixer is not None:
        o, wo, bo, go = mixer
        args += [o, wo, bo, go]
        in_specs += [tile(o.shape[1]), _const_spec(wo.shape), _const_spec(bo.shape), _const_spec(go.shape)]
    consts = [g2, w1, w3, w2]
    layer_spec = lambda a: pl.BlockSpec((1,) + a.shape[1:], lambda i: (layer, 0, 0), pipeline_mode=pl.Buffered(1))
    return pl.pallas_call(
        functools.partial(_ffn_body, mixer_proj=mixer is not None),
        grid=(t // tm,),
        in_specs=in_specs + [_const_spec(g2.shape), layer_spec(w1), layer_spec(w3), layer_spec(w2)],
        out_specs=tile(d),
        out_shape=jax.ShapeDtypeStruct((t, d), F32),
        scratch_shapes=[pltpu.VMEM((tm, d), F32)],
        compiler_params=_params(("arbitrary",)),
        name="ffn",
    )(*args, *consts)


def _gla_body(*refs, tl, valid, has_s0, n_prev, multi_batch):
    x_ref, rest = refs[0], refs[1:]
    s0_ref = prev_ref = None
    if has_s0:
        s0_ref, rest = rest[0], rest[1:]
    if n_prev:
        prev_ref, rest = rest[0], rest[1:]
    (g_ref, wq, wk, wv, wr, wg1, wg2, bg, gn, wo,
     o_ref, s_ref, q_s, k_s, lg_s, v_s, gt_s, og_s, cum_s, qe_s, ke_s, kd_s, oi_s, u_s) = rest[:24]
    c = GLA_CHUNK
    rows = q_s.shape[0]
    pr = x_ref.shape[0] * tl // rows
    n_chunks = rows // c
    j = pl.program_id(1)

    x = x_ref[...]
    @pl.when(j == 0)
    def _():
        if n_prev:
            s_ref[0:n_prev] = prev_ref[...]
        if has_s0:
            s_ref[n_prev] = s0_ref[0]
        else:
            s_ref[n_prev] = jnp.zeros(s_ref.shape[1:], F32)

    h = _rms(x, g_ref[0:1, :]).astype(BF16)
    q = _dot(h, wq[...]) * (GLA_DK ** -0.5)
    k = _dot(h, wk[...])
    z = _dot(_dot(h, wg1[...]).astype(BF16), wg2[...]) + bg[...]
    lg = _log_sigmoid(z) * (1.0 / GLA_TAU)
    if valid < pr:
        assert pr & (pr - 1) == 0
        row = lax.broadcasted_iota(jnp.int32, (x.shape[0], 1), 0) & (pr - 1)
        keep = row < valid
        k = jnp.where(keep, k, 0.0)
        lg = jnp.where(keep, lg, 0.0)
    v = _dot(h, wv[...])
    gt = _silu(_dot(h, wr[...]))
    if pr == tl:
        q_s[...] = q
        k_s[...] = k
        lg_s[...] = lg
        v_s[...] = v.astype(BF16)
        gt_s[...] = gt
    else:
        vf_s = rest[24]
        slots = (q_s, k_s, lg_s, gt_s, vf_s)

        @pl.when((pl.program_id(0) == 0) & (j == 0))
        def _():
            for ref in slots:
                ref[...] = jnp.zeros(ref.shape, F32)

        for b in range(rows // tl):
            for ref, val in zip(slots, (q, k, lg, gt, v)):
                ref[b * tl:b * tl + pr, :] = val[b * pr:(b + 1) * pr, :]
        v_s[...] = vf_s[...].astype(BF16)

    ri = lax.broadcasted_iota(jnp.int32, (c, c), 0)
    ci = lax.broadcasted_iota(jnp.int32, (c, c), 1)
    causal = ri >= ci
    tri = jnp.where(causal, 1.0, 0.0).astype(BF16)
    eye = (lax.broadcasted_iota(jnp.int32, (GLA_DK, GLA_DK), 0)
           == lax.broadcasted_iota(jnp.int32, (GLA_DK, GLA_DK), 1))

    chunk_rows = [slice(ch * c, (ch + 1) * c) for ch in range(n_chunks)]
    head_k = [slice(hd * GLA_DK, (hd + 1) * GLA_DK) for hd in range(GLA_HEADS)]
    head_v = [slice(hd * GLA_DV, (hd + 1) * GLA_DV) for hd in range(GLA_HEADS)]

    for rs in chunk_rows:
        cum_s[rs, :] = _dot3_left(tri, lg_s[rs, :])
    cum = cum_s[...]
    qe_s[...] = (q_s[...] * jnp.exp(cum)).astype(BF16)
    ke_s[...] = (k_s[...] * jnp.exp(-cum)).astype(BF16)
    for ch, rs in enumerate(chunk_rows):
        last = cum_s[(ch + 1) * c - 1:(ch + 1) * c, :]
        kd_s[rs, :] = (k_s[rs, :] * jnp.exp(last - cum_s[rs, :])).astype(BF16)
    for ch, rs in enumerate(chunk_rows):
        for hd in range(GLA_HEADS):
            ks, vs = head_k[hd], head_v[hd]
            att = jnp.where(causal, _dot_nt(qe_s[rs, ks], ke_s[rs, ks]), 0.0).astype(BF16)
            vc = v_s[rs, vs]
            oi_s[rs, vs] = _dot(att, vc)
            u_s[ch, hd] = _dot_tn(kd_s[rs, ks], vc)

    for ch, rs in enumerate(chunk_rows):
        nb = ch if multi_batch else 0
        for hd in range(GLA_HEADS):
            ks, vs = head_k[hd], head_v[hd]
            s_old = s_ref[n_prev, nb, hd]
            o = oi_s[rs, vs] + _dot(qe_s[rs, ks], s_old.astype(BF16))
            last = cum_s[(ch + 1) * c - 1:(ch + 1) * c, ks]
            dcol = jnp.sum(jnp.where(eye, jnp.exp(last), 0.0), axis=1, keepdims=True)
            s_ref[n_prev, nb, hd] = s_old * dcol + u_s[ch, hd]
            on = o * lax.rsqrt(jnp.mean(o * o, axis=-1, keepdims=True) + RMS_EPS) * gn[...]
            og_s[rs, vs] = (on * gt_s[rs, vs]).astype(BF16)
    y = _dot(og_s[...], wo[...])
    if pr == tl:
        o_ref[...] = x + _rms(y, g_ref[1:2, :])
    else:
        for b in range(rows // tl):
            o_ref[b * pr:(b + 1) * pr, :] = (x[b * pr:(b + 1) * pr, :]
                                             + _rms(y[b * tl:b * tl + pr, :], g_ref[1:2, :]))


def _gla_layer(x, s0, prev, g2, w, *, n_batch, seq, nb, tl, valid, pr=None):
    t, d = x.shape
    rows = nb * tl
    nl = seq // tl
    assert nb == 1 or nl == 1
    hk, hv = GLA_HEADS * GLA_DK, GLA_HEADS * GLA_DV
    has_s0 = s0 is not None
    n_prev = 0 if prev is None else prev.shape[0]
    st_spec = lambda n: pl.BlockSpec((n, nb, GLA_HEADS, GLA_DK, GLA_DV), lambda i, j: (0, i, 0, 0, 0))
    pr = tl if pr is None else pr
    assert pr == tl or nl == 1
    x_spec = pl.BlockSpec((nb * pr, d), lambda i, j: (i * nl + j, 0))
    in_specs = [x_spec]
    args = [x]
    if has_s0:
        s0_all, s0_layer = s0
        in_specs.append(pl.BlockSpec((1, nb, GLA_HEADS, GLA_DK, GLA_DV), lambda i, j: (s0_layer, i, 0, 0, 0)))
        args.append(s0_all)
    if n_prev:
        in_specs.append(st_spec(n_prev))
        args.append(prev)
    weights = [g2, w["wq"], w["wk"], w["wv"], w["wr"], w["wg1"], w["wg2"], w["bg"], w["gn"], w["wo"]]
    in_specs += [_const_spec(a.shape) for a in weights]
    args += weights
    body = functools.partial(_gla_body, tl=tl, valid=valid, has_s0=has_s0, n_prev=n_prev, multi_batch=nb > 1)
    return pl.pallas_call(
        body,
        grid=(n_batch // nb, nl),
        in_specs=in_specs,
        out_specs=[x_spec, st_spec(n_prev + 1)],
        out_shape=[jax.ShapeDtypeStruct((t, d), F32),
                   jax.ShapeDtypeStruct((n_prev + 1, n_batch, GLA_HEADS, GLA_DK, GLA_DV), F32)],
        scratch_shapes=[pltpu.VMEM((rows, hk), F32), pltpu.VMEM((rows, hk), F32), pltpu.VMEM((rows, hk), F32),
                        pltpu.VMEM((rows, hv), BF16), pltpu.VMEM((rows, hv), F32), pltpu.VMEM((rows, hv), BF16),
                        pltpu.VMEM((rows, hk), F32), pltpu.VMEM((rows, hk), BF16), pltpu.VMEM((rows, hk), BF16),
                        pltpu.VMEM((rows, hk), BF16), pltpu.VMEM((rows, hv), F32),
                        pltpu.VMEM((rows // GLA_CHUNK, GLA_HEADS, GLA_DK, GLA_DV), F32)]
        + ([pltpu.VMEM((rows, hv), F32)] if pr < tl else []),
        compiler_params=_params(("arbitrary", "arbitrary")),
        name="gla_layer",
    )(*args)


def _fox_bias_tables():
    pq = np.zeros((3 * LANES, D_MODEL), np.float32)
    pk = np.zeros((3 * LANES, D_MODEL), np.float32)
    oq = np.zeros((1, D_MODEL), np.float32)
    ok = np.zeros((1, D_MODEL), np.float32)
    for hh in range(FOX_HEADS):
        for i in range(3):
            pq[i * LANES + hh, hh * FOX_DH + i] = 1.0
            ok[0, hh * FOX_DH + i] = 1.0
            oq[0, hh * FOX_DH + 3 + i] = 1.0
            pk[i * LANES + hh, hh * FOX_DH + 3 + i] = -1.0
    return (jnp.asarray(pq, BF16), jnp.asarray(pk, BF16), jnp.asarray(oq, F32), jnp.asarray(ok, F32))


def _fox_proj_prompt_body(x_ref, g_ref, wq, wk, wv, wf, bf, pq, pk, oq, ok, tri_ref,
                          kt_out, vt_out, lf_out, q2_out, k2_out, vbt_out, carry_ref):
    tl = x_ref.shape[0]

    @pl.when(pl.program_id(1) == 0)
    def _():
        carry_ref[...] = jnp.zeros(carry_ref.shape, F32)

    h = _rms(x_ref[...], g_ref[...]).astype(BF16)
    q = (_dot(h, wq[...]) * (FOX_SCALE * LOG2E)).astype(BF16)
    k = _dot(h, wk[...])
    v = _dot(h, wv[...])
    kt_out[0] = k.T
    vt = v.T
    vt_out[0] = vt
    vbt_out[0, 0] = vt.astype(BF16)
    lf = _log_sigmoid(_dot(h, wf[...]) + bf[...])
    lf_out[...] = lf[:, :FOX_HEADS]
    cum = _dot3_left(tri_ref[...], lf) + carry_ref[...]
    carry_ref[...] = cum[tl - 1:tl, :]
    c3 = jnp.concatenate(_split3(cum * LOG2E), axis=1)
    qb = (_dot(c3, pq[...]) + oq[...]).astype(BF16)
    kb = (_dot(c3, pk[...]) + ok[...]).astype(BF16)
    kh = k.astype(BF16)
    for hp in range(FOX_HEADS // 2):
        a, b = hp * LANES, (hp + 1) * LANES
        q2_out[:, 2 * a:2 * a + LANES] = q[:, a:b]
        q2_out[:, 2 * a + LANES:2 * b] = qb[:, a:b]
        k2_out[:, 2 * a:2 * a + LANES] = kh[:, a:b]
        k2_out[:, 2 * a + LANES:2 * b] = kb[:, a:b]


def _fox_proj_prompt(x, g, w, tabs, *, n_batch, seq, tl):
    t, d = x.shape
    nl = seq // tl
    pq, pk, oq, ok = tabs
    tri = jnp.asarray(np.tril(np.ones((tl, tl), np.float32)), BF16)
    consts = [g, w["wq"], w["wk"], w["wv"], w["wf"], w["bf"], pq, pk, oq, ok, tri]
    row = lambda i, j: (i * nl + j, 0)
    return pl.pallas_call(
        _fox_proj_prompt_body,
        grid=(n_batch, nl),
        in_specs=[pl.BlockSpec((tl, d), row)] + [_const_spec(a.shape) for a in consts],
        out_specs=[pl.BlockSpec((1, d, tl), lambda i, j: (i, 0, j)), pl.BlockSpec((1, d, tl), lambda i, j: (i, 0, j)),
                   pl.BlockSpec((tl, FOX_HEADS), row),
                   pl.BlockSpec((tl, 2 * d), row), pl.BlockSpec((tl, 2 * d), row),
                   pl.BlockSpec((1, 1, d, tl), lambda i, j: (i, j, 0, 0))],
        out_shape=[jax.ShapeDtypeStruct((n_batch, d, seq), F32), jax.ShapeDtypeStruct((n_batch, d, seq), F32),
                   jax.ShapeDtypeStruct((t, FOX_HEADS), F32),
                   jax.ShapeDtypeStruct((t, 2 * d), BF16), jax.ShapeDtypeStruct((t, 2 * d), BF16),
                   jax.ShapeDtypeStruct((n_batch, nl, d, tl), BF16)],
        scratch_shapes=[pltpu.VMEM((1, LANES), F32)],
        compiler_params=_params(("arbitrary", "arbitrary")),
        name="fox_proj_prompt",
    )(x, *consts)


def _fold8(x, op):
    n = x.shape[0]
    while n > 8:
        n //= 2
        x = op(x[:n], x[n:])
    return x


def _fox_attn_body(q_ref, k_ref, vt_ref, o_ref, s_s, p_s, m_s, l_s, a_s, acc_s, *, tq):
    nq = q_ref.shape[0] // tq
    lane2 = lax.broadcasted_iota(jnp.int32, (1, 2 * LANES), 1) & (LANES - 1)
    sel = [jnp.where(lane2 < FOX_DH, 1.0, 0.0).astype(BF16),
           jnp.where(lane2 >= FOX_DH, 1.0, 0.0).astype(BF16)]
    row = lax.broadcasted_iota(jnp.int32, (LANES, 1), 0)
    steps = [(qi, kj) for qi in range(nq) for kj in range(qi + 1)]

    def logits(i):
        qi, kj = steps[i]
        kb = k_ref[kj * tq:(kj + 1) * tq, :]
        for hh in range(2):
            s_s[hh] = _dot_nt(kb * sel[hh], q_ref[qi * tq:(qi + 1) * tq, :])

    def softmax_pv(i):
        qi, kj = steps[i]
        diag = kj == qi
        if kj == 0:
            m_s[...] = jnp.full(m_s.shape, NEG_INF, F32)
            l_s[...] = jnp.zeros(l_s.shape, F32)
            acc_s[...] = jnp.zeros(acc_s.shape, F32)
        vt = vt_ref[0, kj]
        for hh in range(2):
            for c0 in range(0, tq, LANES):
                cs = slice(c0, c0 + LANES)
                def load_s(r0):
                    s = s_s[hh, r0:r0 + LANES, cs]
                    if diag and r0 >= c0:
                        key = lax.broadcasted_iota(jnp.int32, (LANES, LANES), 0) + r0
                        qry = lax.broadcasted_iota(jnp.int32, (LANES, LANES), 1) + c0
                        s = jnp.where(key <= qry, s, NEG_INF)
                    return s

                mx = _fold8(load_s(0), jnp.maximum)
                for r0 in range(LANES, tq, LANES):
                    mx = jnp.maximum(mx, _fold8(load_s(r0), jnp.maximum))
                m_old = m_s[hh, :, cs]
                m_new = jnp.maximum(m_old, jnp.max(mx, axis=0, keepdims=True))
                alpha = jnp.exp2(m_old - m_new)
                sm = None
                for r0 in range(0, tq, LANES):
                    p = jnp.exp2(load_s(r0) - m_new)
                    part = _fold8(p, jnp.add)
                    sm = part if sm is None else sm + part
                    p_s[hh, r0:r0 + LANES, cs] = p.astype(BF16)
                l_s[hh, :, cs] = alpha * l_s[hh, :, cs] + jnp.sum(sm, axis=0, keepdims=True)
                m_s[hh, :, cs] = m_new
                a_s[hh, :, cs] = alpha
            acc_s[hh] = a_s[hh] * acc_s[hh] + _dot(vt, p_s[hh])
        if diag:
            o_t = jnp.where(row < FOX_DH, acc_s[0] / l_s[0], acc_s[1] / l_s[1])
            o_ref[qi * tq:(qi + 1) * tq, :] = o_t.T.astype(BF16)

    for i in range(len(steps)):
        logits(i)
        softmax_pv(i)


def _fox_attn(q2, k2, vbt, *, n_batch, seq, tq):
    t = q2.shape[0]
    nq = seq // tq
    npair = FOX_HEADS // 2
    return pl.pallas_call(
        functools.partial(_fox_attn_body, tq=tq),
        grid=(n_batch, npair),
        in_specs=[pl.BlockSpec((seq, 2 * LANES), lambda b, p: (b, p)),
                  pl.BlockSpec((seq, 2 * LANES), lambda b, p: (b, p)),
                  pl.BlockSpec((1, nq, LANES, tq), lambda b, p: (b, 0, p, 0))],
        out_specs=pl.BlockSpec((seq, LANES), lambda b, p: (b, p)),
        out_shape=jax.ShapeDtypeStruct((t, D_MODEL), BF16),
        scratch_shapes=[pltpu.VMEM((2, tq, tq), F32), pltpu.VMEM((2, tq, tq), BF16),
                        pltpu.VMEM((2, 1, tq), F32), pltpu.VMEM((2, 1, tq), F32), pltpu.VMEM((2, 1, tq), F32),
                        pltpu.VMEM((2, LANES, tq), F32)],
        compiler_params=_params(("arbitrary", "arbitrary")),
        name="fox_attn",
    )(q2, k2, vbt)


def _fox_proj_sample_body(x_ref, g_ref, wq, wk, wv, wf, bf, q_out, k_out, v_out, lf_out):
    h = _rms(x_ref[...], g_ref[...]).astype(BF16)
    q_out[...] = _dot(h, wq[...]) * FOX_SCALE
    k_out[...] = _dot(h, wk[...])
    v_out[...] = _dot(h, wv[...])
    lf_out[...] = _log_sigmoid(_dot(h, wf[...]) + bf[...])[:, :FOX_HEADS]


def _fox_proj_sample(x, g, w):
    t, d = x.shape
    consts = [g, w["wq"], w["wk"], w["wv"], w["wf"], w["bf"]]
    full = lambda n: pl.BlockSpec((t, n), lambda i: (0, 0))
    return pl.pallas_call(
        _fox_proj_sample_body,
        grid=(1,),
        in_specs=[full(d)] + [_const_spec(a.shape) for a in consts],
        out_specs=[full(d), full(d), full(d), full(FOX_HEADS)],
        out_shape=[jax.ShapeDtypeStruct((t, d), F32)] * 3 + [jax.ShapeDtypeStruct((t, FOX_HEADS), F32)],
        compiler_params=_params(("arbitrary",)),
        name="fox_proj_sample",
    )(x, *consts)


def _page_suffix_body(lf_ref, out_ref):
    ji = lax.broadcasted_iota(jnp.int32, (PAGE_SIZE, PAGE_SIZE), 0)
    si = lax.broadcasted_iota(jnp.int32, (PAGE_SIZE, PAGE_SIZE), 1)
    later = jnp.where(ji > si, 1.0, 0.0).astype(BF16)
    ones = jnp.ones((PAGE_SIZE, PAGE_SIZE), BF16)
    hi, mid, lo = _split3(lf_ref[...])
    pages = out_ref.shape[0]
    sfx = _dot(hi, later) + _dot(mid, later) + _dot(lo, later)
    tot = _dot(hi, ones) + _dot(mid, ones) + _dot(lo, ones)
    out_ref[:, 0:FOX_HEADS, :] = sfx.reshape(pages, FOX_HEADS, PAGE_SIZE)
    out_ref[:, FOX_HEADS:2 * FOX_HEADS, :] = tot.reshape(pages, FOX_HEADS, PAGE_SIZE)


def _page_suffix(lf_t, pages_per_step):
    r, p = lf_t.shape
    n_pool = r // FOX_HEADS
    return pl.pallas_call(
        _page_suffix_body,
        grid=(n_pool // pages_per_step,),
        in_specs=[pl.BlockSpec((pages_per_step * FOX_HEADS, p), lambda i: (i, 0))],
        out_specs=pl.BlockSpec((pages_per_step, 2 * FOX_HEADS, p), lambda i: (i, 0, 0)),
        out_shape=jax.ShapeDtypeStruct((n_pool, 2 * FOX_HEADS, p), F32),
        compiler_params=_params(("arbitrary",)),
        name="page_suffix",
    )(lf_t)


def _fox_decode_body(pt_ref, q_ref, kn_ref, vn_ref, lfn_ref, *refs, n_q, pages_per_step):
    pp = pages_per_step
    page_refs = refs[:3 * pp]
    o_ref, m_s, l_s, acc_s, run_s, ccol_s, qbd_s = refs[3 * pp:]
    del pt_ref
    j = pl.program_id(1)
    hq = FOX_HEADS * n_q
    hrow = lax.broadcasted_iota(jnp.int32, (FOX_HEADS, D_MODEL), 0)
    hcol = lax.broadcasted_iota(jnp.int32, (FOX_HEADS, D_MODEL), 1) >> 6
    own = hrow == hcol

    def attend(s_parts, vt_parts):
        s = s_parts[0] if len(s_parts) == 1 else jnp.concatenate(s_parts, axis=1)
        m_old = m_s[...]
        m_new = jnp.maximum(m_old, jnp.max(s, axis=1, keepdims=True))
        alpha = jnp.exp(m_old - m_new)
        p = jnp.exp(s - m_new)
        l_s[...] = alpha * l_s[...] + jnp.sum(p, axis=1, keepdims=True)
        pb = p.astype(BF16)
        pv = _dot_nt(pb[:, 0:PAGE_SIZE], vt_parts[0])
        for i in range(1, len(vt_parts)):
            pv = pv + _dot_nt(pb[:, i * PAGE_SIZE:(i + 1) * PAGE_SIZE], vt_parts[i])
        acc_s[...] = alpha * acc_s[...] + pv
        m_s[...] = m_new

    def tile_q(bias16):
        return jnp.concatenate([bias16] * n_q, axis=0)

    @pl.when(j == 0)
    def _():
        lfn = lfn_ref[0]
        cn = lfn
        for sh in range(1, n_q):
            cn = cn + pltpu.roll(lfn, sh, axis=1)
        ccol = jnp.concatenate(
            [jnp.broadcast_to(cn[:, qq:qq + 1], (FOX_HEADS, LANES)) for qq in range(n_q)], axis=0)
        ccol_s[...] = ccol
        qbd_s[...] = jnp.concatenate(
            [jnp.where(own, jnp.broadcast_to(q_ref[0, qq:qq + 1, :], (FOX_HEADS, D_MODEL)), 0.0)
             for qq in range(n_q)], axis=0).astype(BF16)
        m_s[...] = jnp.full(m_s.shape, NEG_INF, F32)
        l_s[...] = jnp.zeros(l_s.shape, F32)
        acc_s[...] = jnp.zeros(acc_s.shape, F32)
        run_s[...] = jnp.zeros(run_s.shape, F32)
        qrow = lax.broadcasted_iota(jnp.int32, (hq, LANES), 0) >> 4
        key = lax.broadcasted_iota(jnp.int32, (hq, LANES), 1)
        s_new = _dot(qbd_s[...], kn_ref[0]) + tile_q(-cn) + ccol
        attend([jnp.where(key <= qrow, s_new, NEG_INF)], [vn_ref[0]])

    run = run_s[...]
    ccol = ccol_s[...]
    qbd = qbd_s[...]
    s_parts, vt_parts = [], []
    for i in range(pp):
        kt_ref, vt_ref, b_ref = page_refs[3 * i:3 * i + 3]
        bias16 = b_ref[0, 0:FOX_HEADS, :] + run
        s_parts.append(_dot(qbd, kt_ref[0].astype(BF16)) + tile_q(bias16) + ccol)
        run = run + b_ref[0, FOX_HEADS:2 * FOX_HEADS, :]
        vt_parts.append(vt_ref[0].astype(BF16))
    run_s[...] = run
    attend(s_parts, vt_parts)

    @pl.when(j == pl.num_programs(1) - 1)
    def _():
        res = acc_s[...] / l_s[...]
        rows = [jnp.sum(jnp.where(own, res[qq * FOX_HEADS:(qq + 1) * FOX_HEADS, :], 0.0), axis=0, keepdims=True)
                for qq in range(n_q)]
        o_ref[0] = jnp.concatenate(rows, axis=0)


def _fox_decode(page_table, q, kt_new, vt_new, lf_new_t, kt_pool, vt_pool, bias_pool, *, n_q, pages_per_step):
    n_batch, n_pages = page_table.shape
    pp = pages_per_step
    hq = FOX_HEADS * n_q
    d = D_MODEL
    steps = n_pages // pp
    pt = page_table.reshape(-1)

    def page_map(i):
        def f(b, j, pt_ref):
            return (pt_ref[b * n_pages + (n_pages - 1 - (j * pp + i))], 0, 0)
        return f

    per_batch = lambda shape: pl.BlockSpec((1,) + shape, lambda b, j, pt_ref: (b, 0, 0))
    in_specs = [per_batch((n_q, d)), per_batch((d, PAGE_SIZE)), per_batch((d, PAGE_SIZE)),
                per_batch((FOX_HEADS, LANES))]
    args = [q, kt_new, vt_new, lf_new_t]
    for i in range(pp):
        in_specs += [pl.BlockSpec((1, d, PAGE_SIZE), page_map(i)), pl.BlockSpec((1, d, PAGE_SIZE), page_map(i)),
                     pl.BlockSpec((1, 2 * FOX_HEADS, PAGE_SIZE), page_map(i))]
        args += [kt_pool, vt_pool, bias_pool]
    grid_spec = pltpu.PrefetchScalarGridSpec(
        num_scalar_prefetch=1,
        grid=(n_batch, steps),
        in_specs=in_specs,
        out_specs=pl.BlockSpec((1, n_q, d), lambda b, j, pt_ref: (b, 0, 0)),
        scratch_shapes=[pltpu.VMEM((hq, 1), F32), pltpu.VMEM((hq, 1), F32), pltpu.VMEM((hq, d), F32),
                        pltpu.VMEM((FOX_HEADS, LANES), F32), pltpu.VMEM((hq, LANES), F32),
                        pltpu.VMEM((hq, d), BF16)],
    )
    return pl.pallas_call(
        functools.partial(_fox_decode_body, n_q=n_q, pages_per_step=pp),
        grid_spec=grid_spec,
        out_shape=jax.ShapeDtypeStruct((n_batch, n_q, d), F32),
        compiler_params=_params(("arbitrary", "arbitrary")),
        name="fox_decode",
    )(pt, *args)


def _conv_tail(y, x, g_post, lng, lnb, w2, b2):
    mu = jnp.mean(y, axis=-1, keepdims=True)
    yc = y - mu
    var = jnp.mean(yc * yc, axis=-1, keepdims=True)
    yn = yc * lax.rsqrt(var + LN_EPS) * lng + lnb
    m = _dot(_silu(yn).astype(BF16), w2) + b2
    return x + _rms(m, g_post)


def _glu_in(x, g_pre, w1, b1):
    h = _rms(x, g_pre).astype(BF16)
    u2 = _dot(h, w1) + b1
    return u2[:, :D_MODEL] * jax.nn.sigmoid(u2[:, D_MODEL:])


_CONV_RB = 128


def _conv_prompt_body(x_ref, g_ref, w1, b1, wdw, bdw, lng, lnb, w2, b2, o_ref, st_ref, ext_ref, y_ref):
    tl = x_ref.shape[0]
    l = pl.program_id(1)
    x = x_ref[...]

    @pl.when(l == 0)
    def _():
        ext_ref[0:CONV_HDR, :] = jnp.zeros((CONV_HDR, D_MODEL), F32)

    ext_ref[CONV_HDR:CONV_HDR + tl, :] = _glu_in(x, g_ref[0:1, :], w1[...], b1[...])
    base = CONV_HDR - (CONV_WIDTH - 1)
    sub = 8
    for r0 in range(0, tl, _CONV_RB):
        for c0 in range(0, D_MODEL, LANES):
            cs = slice(c0, c0 + LANES)
            acc = jnp.broadcast_to(bdw[:, cs], (_CONV_RB, LANES))
            for s in range(sub):
                n = _CONV_RB if s == 0 else _CONV_RB + sub
                z = None
                for a in range((base + CONV_WIDTH - 1) // sub + 1):
                    wi = sub * a + s - base
                    if 0 <= wi < CONV_WIDTH:
                        term = ext_ref[r0 + sub * a:r0 + sub * a + n, cs] * wdw[wi:wi + 1, cs]
                        z = term if z is None else z + term
                acc = acc + z[s:s + _CONV_RB]
            y_ref[r0:r0 + _CONV_RB, cs] = acc
    o_ref[...] = _conv_tail(y_ref[...], x, g_ref[1:2, :], lng[...], lnb[...], w2[...], b2[...])

    @pl.when(l == pl.num_programs(1) - 1)
    def _():
        st_ref[0] = ext_ref[CONV_HDR + tl - (CONV_WIDTH - 1):CONV_HDR + tl, :]

    ext_ref[0:CONV_HDR, :] = ext_ref[tl:tl + CONV_HDR, :]


def _conv_prompt(x, g2, w, *, n_batch, seq, tl):
    t, d = x.shape
    nl = seq // tl
    consts = [g2, w["w1"], w["b1"], w["wdw"], w["bdw"], w["lng"], w["lnb"], w["w2"], w["b2"]]
    row = lambda i, j: (i * nl + j, 0)
    return pl.pallas_call(
        _conv_prompt_body,
        grid=(n_batch, nl),
        in_specs=[pl.BlockSpec((tl, d), row)] + [_const_spec(a.shape) for a in consts],
        out_specs=[pl.BlockSpec((tl, d), row),
                   pl.BlockSpec((1, CONV_WIDTH - 1, d), lambda i, j: (i, 0, 0))],
        out_shape=[jax.ShapeDtypeStruct((t, d), F32),
                   jax.ShapeDtypeStruct((n_batch, CONV_WIDTH - 1, d), F32)],
        scratch_shapes=[pltpu.VMEM((CONV_HDR + tl, d), F32), pltpu.VMEM((tl, d), F32)],
        compiler_params=_params(("arbitrary", "arbitrary")),
        name="conv_prompt",
    )(x, *consts)


_SAMPLE_PAD = 8


def _conv_sample_body(x_ref, st_in, g_ref, w1, b1, wdw, bdw, lng, lnb, w2, b2, o_ref, st_out, ext_ref, u_ref, y_ref,
                      *, n_q):
    n_batch = st_in.shape[0]
    x = x_ref[...]
    u_ref[...] = _glu_in(x, g_ref[0:1, :], w1[...], b1[...])
    base = CONV_HDR - (CONV_WIDTH - 1)

    def per_batch(b, carry):
        r0 = pl.multiple_of(b * _SAMPLE_PAD, _SAMPLE_PAD)
        ext_ref[0:CONV_HDR, :] = st_in[b]
        ext_ref[CONV_HDR:CONV_HDR + _SAMPLE_PAD, :] = u_ref[pl.ds(r0, _SAMPLE_PAD), :]
        acc = jnp.broadcast_to(bdw[...], (_SAMPLE_PAD, D_MODEL))
        for wi in range(CONV_WIDTH):
            acc = acc + ext_ref[base + wi:base + wi + _SAMPLE_PAD, :] * wdw[wi:wi + 1, :]
        y_ref[pl.ds(r0, _SAMPLE_PAD), :] = acc
        st_out[b] = ext_ref[base + n_q:base + n_q + CONV_WIDTH - 1, :]
        return carry

    lax.fori_loop(0, n_batch, per_batch, 0)
    o_ref[...] = _conv_tail(y_ref[...], x, g_ref[1:2, :], lng[...], lnb[...], w2[...], b2[...])


def _conv_sample(x_pad, st_pad, g2, w, *, n_q):
    t, d = x_pad.shape
    n_batch = st_pad.shape[0]
    consts = [g2, w["w1"], w["b1"], w["wdw"], w["bdw"], w["lng"], w["lnb"], w["w2"], w["b2"]]
    return pl.pallas_call(
        functools.partial(_conv_sample_body, n_q=n_q),
        grid=(1,),
        in_specs=[pl.BlockSpec((t, d), lambda i: (0, 0)),
                  pl.BlockSpec(st_pad.shape, lambda i: (0, 0, 0))] + [_const_spec(a.shape) for a in consts],
        out_specs=[pl.BlockSpec((t, d), lambda i: (0, 0)),
                   pl.BlockSpec((n_batch, CONV_WIDTH - 1, d), lambda i: (0, 0, 0))],
        out_shape=[jax.ShapeDtypeStruct((t, d), F32),
                   jax.ShapeDtypeStruct((n_batch, CONV_WIDTH - 1, d), F32)],
        scratch_shapes=[pltpu.VMEM((CONV_HDR + _SAMPLE_PAD, d), F32), pltpu.VMEM((t, d), F32),
                        pltpu.VMEM((t, d), F32)],
        compiler_params=_params(("arbitrary",)),
        name="conv_sample",
    )(x_pad, st_pad, *consts)


def _row(v):
    return v.reshape(1, -1).astype(F32)


def _pad_cols(a, n):
    return jnp.pad(a, ((0, 0), (0, n - a.shape[1])))


def _gla_weights(wq, wk, wv, wg1, wg2, bg, wr, gn, wo):
    rank = wg1.shape[1]
    return dict(wq=wq.astype(BF16), wk=wk.astype(BF16), wv=wv.astype(BF16), wr=wr.astype(BF16),
                wg1=_pad_cols(wg1, LANES).astype(BF16),
                wg2=jnp.pad(wg2, ((0, LANES - rank), (0, 0))).astype(BF16),
                bg=_row(bg), gn=_row(gn), wo=wo.astype(BF16))


def _gla_sample(xs, s0, prev, g2, w, *, n_batch, n_q):
    d = xs.shape[1]
    c = GLA_CHUNK
    pr = _SAMPLE_PAD
    xpad = jnp.pad(xs.reshape(n_batch, n_q, d), ((0, 0), (0, pr - n_q), (0, 0))).reshape(n_batch * pr, d)
    out, s_fin = _gla_layer(xpad, s0, prev, g2, w, n_batch=n_batch, seq=c, nb=4, tl=c, valid=n_q, pr=pr)
    return out.reshape(n_batch, pr, d)[:, :n_q].reshape(n_batch * n_q, d), s_fin


def _fox_sample(xs, g_pre, w, k_pool, v_pool, lf_pool, page_table, *, n_batch, n_q):
    d = D_MODEL
    q, k_new, v_new, lf_new = _fox_proj_sample(xs, g_pre, w)
    new_t = lambda a: jnp.pad(a.reshape(n_batch, n_q, d).transpose(0, 2, 1),
                              ((0, 0), (0, 0), (0, PAGE_SIZE - n_q))).astype(BF16)
    lf_new_t = jnp.pad(lf_new.reshape(n_batch, n_q, FOX_HEADS).transpose(0, 2, 1),
                       ((0, 0), (0, 0), (0, LANES - n_q)))
    n_pool = lf_pool.shape[0]
    pool_t = lambda a: a.transpose(0, 2, 3, 1).reshape(n_pool, d, PAGE_SIZE)
    lf_t = lf_pool.transpose(0, 2, 1).reshape(n_pool * FOX_HEADS, PAGE_SIZE)
    bias_pool = _page_suffix(lf_t, pages_per_step=n_pool // 8)
    o = _fox_decode(page_table, q.reshape(n_batch, n_q, d), new_t(k_new), new_t(v_new), lf_new_t,
                    pool_t(k_pool), pool_t(v_pool), bias_pool, n_q=n_q,
                    pages_per_step=min(16, page_table.shape[1]))
    return o.reshape(n_batch * n_q, d), k_new, v_new, lf_new


def kernel(x_prompt, x_sample, state_gla, cache_fox_k, cache_fox_v, cache_fox_logf, state_conv, page_table,
           norm_g, gla_wq, gla_wk, gla_wv, gla_wg1, gla_wg2, gla_bg, gla_wr, gla_gn, gla_wo,
           fox_wq, fox_wk, fox_wv, fox_wf, fox_bf, fox_wo,
           conv_w1, conv_b1, conv_wdw, conv_bdw, conv_ln_g, conv_ln_b, conv_w2, conv_b2,
           ffn_w1, ffn_w3, ffn_w2):
    bp, seq, d = x_prompt.shape
    bs, n_q, _ = x_sample.shape
    xp = x_prompt.reshape(bp * seq, d)
    xs = x_sample.reshape(bs * n_q, d)
    tp = 512
    w1_all, w3_all, w2_all = ffn_w1.astype(BF16), ffn_w3.astype(BF16), ffn_w2.astype(BF16)
    gla_p = gla_s = None
    cv_p, cv_s = [], []
    fox_out = None
    for i in range(DEPTH):
        j = i // N_MIXERS
        kind = i % N_MIXERS
        g_mix = norm_g[i, 0:2].astype(F32)
        mix_p = mix_s = None
        if kind == 0:
            w = _gla_weights(gla_wq[j], gla_wk[j], gla_wv[j], gla_wg1[j], gla_wg2[j], gla_bg[j], gla_wr[j],
                             gla_gn[j], gla_wo[j])
            xp, gla_p = _gla_layer(xp, None, gla_p, g_mix, w, n_batch=bp, seq=seq, nb=1, tl=tp, valid=tp)
            xs, gla_s = _gla_sample(xs, (state_gla, j), gla_s, g_mix, w, n_batch=bs, n_q=n_q)
        elif kind == 1:
            w = dict(wq=fox_wq[j].astype(BF16), wk=fox_wk[j].astype(BF16), wv=fox_wv[j].astype(BF16),
                     wf=_pad_cols(fox_wf[j], LANES).astype(BF16), bf=_pad_cols(_row(fox_bf[j]), LANES))
            wo = fox_wo[j].astype(BF16)
            zero_b = jnp.zeros((1, d), F32)
            g_pre, g_post = g_mix[0:1], g_mix[1:2]
            ktp, vtp, lfp, q2, k2, vbt = _fox_proj_prompt(xp, g_pre, w, _fox_bias_tables(), n_batch=bp, seq=seq, tl=tp)
            op = _fox_attn(q2, k2, vbt, n_batch=bp, seq=seq, tq=tp)
            os_, kn, vn, lfn = _fox_sample(xs, g_pre, w, cache_fox_k[j], cache_fox_v[j], cache_fox_logf[j],
                                           page_table, n_batch=bs, n_q=n_q)
            mix_p, mix_s = (op, wo, zero_b, g_post), (os_, wo, zero_b, g_post)
            cache_view = lambda a: a.reshape(1, bp, FOX_HEADS, FOX_DH, seq).transpose(0, 1, 4, 2, 3)
            fox_out = (cache_view(ktp), cache_view(vtp),
                       lfp.reshape(1, bp, seq, FOX_HEADS),
                       kn.reshape(1, bs, n_q, FOX_HEADS, FOX_DH), vn.reshape(1, bs, n_q, FOX_HEADS, FOX_DH),
                       lfn.reshape(1, bs, n_q, FOX_HEADS))
        else:
            w = dict(w1=conv_w1[j].astype(BF16), b1=_row(conv_b1[j]), wdw=conv_wdw[j].astype(F32),
                     bdw=_row(conv_bdw[j]), lng=_row(conv_ln_g[j]), lnb=_row(conv_ln_b[j]),
                     w2=conv_w2[j].astype(BF16), b2=_row(conv_b2[j]))
            xp, stp = _conv_prompt(xp, g_mix, w, n_batch=bp, seq=seq, tl=tp)
            xs_pad = jnp.pad(xs.reshape(bs, n_q, d), ((0, 0), (0, _SAMPLE_PAD - n_q), (0, 0)))
            st_pad = jnp.pad(state_conv[j], ((0, 0), (CONV_HDR - (CONV_WIDTH - 1), 0), (0, 0)))
            xs_pad, sts = _conv_sample(xs_pad.reshape(bs * _SAMPLE_PAD, d), st_pad, g_mix, w, n_q=n_q)
            xs = xs_pad.reshape(bs, _SAMPLE_PAD, d)[:, :n_q].reshape(bs * n_q, d)
            cv_p.append(stp)
            cv_s.append(sts)
        g_ffn = norm_g[i, 2:4].astype(F32)
        xp = _ffn(xp, g_ffn, w1_all, w3_all, w2_all, i, tp, mixer=mix_p)
        xs = _ffn(xs, g_ffn, w1_all, w3_all, w2_all, i, bs * n_q, mixer=mix_s)
    return (xp.reshape(bp, seq, d), xs.reshape(bs, n_q, d), gla_p, gla_s,
            *fox_out, jnp.stack(cv_p), jnp.stack(cv_s))
```

```python
import functools

import numpy as np
import jax
import jax.numpy as jnp
from jax import lax
from jax.experimental import pallas as pl
from jax.experimental.pallas import tpu as pltpu

F32 = jnp.float32
BF16 = jnp.bfloat16

D_MODEL = 1024
DEPTH = 4
N_MIXERS = 3
GLA_HEADS = 4
GLA_DK = 128
GLA_DV = 256
GLA_TAU = 16.0
GLA_CHUNK = 64
FOX_HEADS = 16
FOX_DH = 64
FOX_SCALE = FOX_DH ** -0.5
LOG2E = 1.4426950408889634
PAGE_SIZE = 128
NEG_INF = -1e30
CONV_WIDTH = 31
CONV_HDR = 32
D_FF = 2816
RMS_EPS = 1e-6
LN_EPS = 1e-5

LANES = 128
VMEM_LIMIT = 52 * 1024 * 1024


def _dot(a, b):
    return jnp.dot(a, b, preferred_element_type=F32)


def _dot_nt(a, b):
    return lax.dot_general(a, b, (((1,), (1,)), ((), ())), preferred_element_type=F32)


def _dot_tn(a, b):
    return lax.dot_general(a, b, (((0,), (0,)), ((), ())), preferred_element_type=F32)


def _rms(x, g):
    return x * lax.rsqrt(jnp.mean(x * x, axis=-1, keepdims=True) + RMS_EPS) * g


def _silu(x):
    return x * jax.nn.sigmoid(x)


def _log_sigmoid(z):
    return jnp.minimum(z, 0.0) - jnp.log1p(jnp.exp(-jnp.abs(z)))


def _split3(a):
    hi = a.astype(BF16)
    r1 = a - hi.astype(F32)
    mid = r1.astype(BF16)
    lo = (r1 - mid.astype(F32)).astype(BF16)
    return hi, mid, lo


def _dot3_left(m, a):
    hi, mid, lo = _split3(a)
    return _dot(m, hi) + _dot(m, mid) + _dot(m, lo)


def _dot3_right(a, m):
    hi, mid, lo = _split3(a)
    return _dot(hi, m) + _dot(mid, m) + _dot(lo, m)


def _const_spec(shape):
    nd = len(shape)
    return pl.BlockSpec(shape, lambda *_: (0,) * nd, pipeline_mode=pl.Buffered(1))


def _params(sem):
    return pltpu.CompilerParams(dimension_semantics=sem, vmem_limit_bytes=VMEM_LIMIT)


_FF_CHUNKS = tuple((s, min(512, D_FF - s)) for s in range(0, D_FF, 512))


def _ffn_body(*refs, mixer_proj):
    if mixer_proj:
        x_ref, mo_ref, wo_ref, bo_ref, go_ref, g_ref, w1_ref, w3_ref, w2_ref, o_ref, acc_ref = refs
        x = x_ref[...] + _rms(_dot(mo_ref[...].astype(BF16), wo_ref[...]) + bo_ref[...], go_ref[...])
    else:
        x_ref, g_ref, w1_ref, w3_ref, w2_ref, o_ref, acc_ref = refs
        x = x_ref[...]
    h = _rms(x, g_ref[0:1, :]).astype(BF16)
    for idx, (s, n) in enumerate(_FF_CHUNKS):
        a = _dot(h, w1_ref[0, :, s:s + n])
        b = _dot(h, w3_ref[0, :, s:s + n])
        u = (_silu(a) * b).astype(BF16)
        y = _dot(u, w2_ref[0, s:s + n, :])
        if idx == 0:
            acc_ref[...] = y
        else:
            acc_ref[...] += y
    o_ref[...] = x + _rms(acc_ref[...], g_ref[1:2, :])


def _ffn(x, g2, w1, w3, w2, layer, tm, mixer=None):
    t, d = x.shape
    tile = lambda n: pl.BlockSpec((tm, n), lambda i: (i, 0))
    args, in_specs = [x], [tile(d)]
    if mixer is not None:
        o, wo, bo, go = mixer
        args += [o, wo, bo, go]
        in_specs += [tile(o.shape[1]), _const_spec(wo.shape), _const_spec(bo.shape), _const_spec(go.shape)]
    consts = [g2, w1, w3, w2]
    layer_spec = lambda a: pl.BlockSpec((1,) + a.shape[1:], lambda i: (layer, 0, 0), pipeline_mode=pl.Buffered(1))
    return pl.pallas_call(
        functools.partial(_ffn_body, mixer_proj=mixer is not None),
        grid=(t // tm,),
        in_specs=in_specs + [_const_spec(g2.shape), layer_spec(w1), layer_spec(w3), layer_spec(w2)],
        out_specs=tile(d),
        out_shape=jax.ShapeDtypeStruct((t, d), F32),
        scratch_shapes=[pltpu.VMEM((tm, d), F32)],
        compiler_params=_params(("arbitrary",)),
        name="ffn",
    )(*args, *consts)


def _gla_body(*refs, tl, valid, has_s0, n_prev, multi_batch):
    x_ref, rest = refs[0], refs[1:]
    s0_ref = prev_ref = None
    if has_s0:
        s0_ref, rest = rest[0], rest[1:]
    if n_prev:
        prev_ref, rest = rest[0], rest[1:]
    (g_ref, wq, wk, wv, wr, wg1, wg2, bg, gn, wo,
     o_ref, s_ref, q_s, k_s, lg_s, v_s, gt_s, og_s, cum_s, qe_s, ke_s, kd_s, oi_s, u_s) = rest[:24]
    c = GLA_CHUNK
    rows = q_s.shape[0]
    pr = x_ref.shape[0] * tl // rows
    n_chunks = rows // c
    j = pl.program_id(1)

    x = x_ref[...]
    @pl.when(j == 0)
    def _():
        if n_prev:
            s_ref[0:n_prev] = prev_ref[...]
        if has_s0:
            s_ref[n_prev] = s0_ref[0]
        else:
            s_ref[n_prev] = jnp.zeros(s_ref.shape[1:], F32)

    h = _rms(x, g_ref[0:1, :]).astype(BF16)
    q = _dot(h, wq[...]) * (GLA_DK ** -0.5)
    k = _dot(h, wk[...])
    z = _dot(_dot(h, wg1[...]).astype(BF16), wg2[...]) + bg[...]
    lg = _log_sigmoid(z) * (1.0 / GLA_TAU)
    if valid < pr:
        assert pr & (pr - 1) == 0
        row = lax.broadcasted_iota(jnp.int32, (x.shape[0], 1), 0) & (pr - 1)
        keep = row < valid
        k = jnp.where(keep, k, 0.0)
        lg = jnp.where(keep, lg, 0.0)
    v = _dot(h, wv[...])
    gt = _silu(_dot(h, wr[...]))
    if pr == tl:
        q_s[...] = q
        k_s[...] = k
        lg_s[...] = lg
        v_s[...] = v.astype(BF16)
        gt_s[...] = gt
    else:
        vf_s = rest[24]
        slots = (q_s, k_s, lg_s, gt_s, vf_s)

        @pl.when((pl.program_id(0) == 0) & (j == 0))
        def _():
            for ref in slots:
                ref[...] = jnp.zeros(ref.shape, F32)

        for b in range(rows // tl):
            for ref, val in zip(slots, (q, k, lg, gt, v)):
                ref[b * tl:b * tl + pr, :] = val[b * pr:(b + 1) * pr, :]
        v_s[...] = vf_s[...].astype(BF16)

    ri = lax.broadcasted_iota(jnp.int32, (c, c), 0)
    ci = lax.broadcasted_iota(jnp.int32, (c, c), 1)
    causal = ri >= ci
    tri = jnp.where(causal, 1.0, 0.0).astype(BF16)
    eye = (lax.broadcasted_iota(jnp.int32, (GLA_DK, GLA_DK), 0)
           == lax.broadcasted_iota(jnp.int32, (GLA_DK, GLA_DK), 1))

    chunk_rows = [slice(ch * c, (ch + 1) * c) for ch in range(n_chunks)]
    head_k = [slice(hd * GLA_DK, (hd + 1) * GLA_DK) for hd in range(GLA_HEADS)]
    head_v = [slice(hd * GLA_DV, (hd + 1) * GLA_DV) for hd in range(GLA_HEADS)]

    for rs in chunk_rows:
        cum_s[rs, :] = _dot3_left(tri, lg_s[rs, :])
    cum = cum_s[...]
    qe_s[...] = (q_s[...] * jnp.exp(cum)).astype(BF16)
    ke_s[...] = (k_s[...] * jnp.exp(-cum)).astype(BF16)
    for ch, rs in enumerate(chunk_rows):
        last = cum_s[(ch + 1) * c - 1:(ch + 1) * c, :]
        kd_s[rs, :] = (k_s[rs, :] * jnp.exp(last - cum_s[rs, :])).astype(BF16)
    for ch, rs in enumerate(chunk_rows):
        for hd in range(GLA_HEADS):
            ks, vs = head_k[hd], head_v[hd]
            att = jnp.where(causal, _dot_nt(qe_s[rs, ks], ke_s[rs, ks]), 0.0).astype(BF16)
            vc = v_s[rs, vs]
            oi_s[rs, vs] = _dot(att, vc)
            u_s[ch, hd] = _dot_tn(kd_s[rs, ks], vc)

    for ch, rs in enumerate(chunk_rows):
        nb = ch if multi_batch else 0
        for hd in range(GLA_HEADS):
            ks, vs = head_k[hd], head_v[hd]
            s_old = s_ref[n_prev, nb, hd]
            o = oi_s[rs, vs] + _dot(qe_s[rs, ks], s_old.astype(BF16))
            last = cum_s[(ch + 1) * c - 1:(ch + 1) * c, ks]
            dcol = jnp.sum(jnp.where(eye, jnp.exp(last), 0.0), axis=1, keepdims=True)
            s_ref[n_prev, nb, hd] = s_old * dcol + u_s[ch, hd]
            on = o * lax.rsqrt(jnp.mean(o * o, axis=-1, keepdims=True) + RMS_EPS) * gn[...]
            og_s[rs, vs] = (on * gt_s[rs, vs]).astype(BF16)
    y = _dot(og_s[...], wo[...])
    if pr == tl:
        o_ref[...] = x + _rms(y, g_ref[1:2, :])
    else:
        for b in range(rows // tl):
            o_ref[b * pr:(b + 1) * pr, :] = (x[b * pr:(b + 1) * pr, :]
                                             + _rms(y[b * tl:b * tl + pr, :], g_ref[1:2, :]))


def _gla_layer(x, s0, prev, g2, w, *, n_batch, seq, nb, tl, valid, pr=None):
    t, d = x.shape
    rows = nb * tl
    nl = seq // tl
    assert nb == 1 or nl == 1
    hk, hv = GLA_HEADS * GLA_DK, GLA_HEADS * GLA_DV
    has_s0 = s0 is not None
    n_prev = 0 if prev is None else prev.shape[0]
    st_spec = lambda n: pl.BlockSpec((n, nb, GLA_HEADS, GLA_DK, GLA_DV), lambda i, j: (0, i, 0, 0, 0))
    pr = tl if pr is None else pr
    assert pr == tl or nl == 1
    x_spec = pl.BlockSpec((nb * pr, d), lambda i, j: (i * nl + j, 0))
    in_specs = [x_spec]
    args = [x]
    if has_s0:
        s0_all, s0_layer = s0
        in_specs.append(pl.BlockSpec((1, nb, GLA_HEADS, GLA_DK, GLA_DV), lambda i, j: (s0_layer, i, 0, 0, 0)))
        args.append(s0_all)
    if n_prev:
        in_specs.append(st_spec(n_prev))
        args.append(prev)
    weights = [g2, w["wq"], w["wk"], w["wv"], w["wr"], w["wg1"], w["wg2"], w["bg"], w["gn"], w["wo"]]
    in_specs += [_const_spec(a.shape) for a in weights]
    args += weights
    body = functools.partial(_gla_body, tl=tl, valid=valid, has_s0=has_s0, n_prev=n_prev, multi_batch=nb > 1)
    return pl.pallas_call(
        body,
        grid=(n_batch // nb, nl),
        in_specs=in_specs,
        out_specs=[x_spec, st_spec(n_prev + 1)],
        out_shape=[jax.ShapeDtypeStruct((t, d), F32),
                   jax.ShapeDtypeStruct((n_prev + 1, n_batch, GLA_HEADS, GLA_DK, GLA_DV), F32)],
        scratch_shapes=[pltpu.VMEM((rows, hk), F32), pltpu.VMEM((rows, hk), F32), pltpu.VMEM((rows, hk), F32),
                        pltpu.VMEM((rows, hv), BF16), pltpu.VMEM((rows, hv), F32), pltpu.VMEM((rows, hv), BF16),
                        pltpu.VMEM((rows, hk), F32), pltpu.VMEM((rows, hk), BF16), pltpu.VMEM((rows, hk), BF16),
                        pltpu.VMEM((rows, hk), BF16), pltpu.VMEM((rows, hv), F32),
                        pltpu.VMEM((rows // GLA_CHUNK, GLA_HEADS, GLA_DK, GLA_DV), F32)]
        + ([pltpu.VMEM((rows, hv), F32)] if pr < tl else []),
        compiler_params=_params(("arbitrary", "arbitrary")),
        name="gla_layer",
    )(*args)


def _fox_bias_tables():
    pq = np.zeros((3 * LANES, D_MODEL), np.float32)
    pk = np.zeros((3 * LANES, D_MODEL), np.float32)
    oq = np.zeros((1, D_MODEL), np.float32)
    ok = np.zeros((1, D_MODEL), np.float32)
    for hh in range(FOX_HEADS):
        for i in range(3):
            pq[i * LANES + hh, hh * FOX_DH + i] = 1.0
            ok[0, hh * FOX_DH + i] = 1.0
            oq[0, hh * FOX_DH + 3 + i] = 1.0
            pk[i * LANES + hh, hh * FOX_DH + 3 + i] = -1.0
    return (jnp.asarray(pq, BF16), jnp.asarray(pk, BF16), jnp.asarray(oq, F32), jnp.asarray(ok, F32))


def _fox_proj_prompt_body(x_ref, g_ref, wq, wk, wv, wf, bf, pq, pk, oq, ok, tri_ref,
                          kt_out, vt_out, lf_out, q2_out, k2_out, vbt_out, carry_ref):
    tl = x_ref.shape[0]

    @pl.when(pl.program_id(1) == 0)
    def _():
        carry_ref[...] = jnp.zeros(carry_ref.shape, F32)

    h = _rms(x_ref[...], g_ref[...]).astype(BF16)
    q = (_dot(h, wq[...]) * (FOX_SCALE * LOG2E)).astype(BF16)
    k = _dot(h, wk[...])
    v = _dot(h, wv[...])
    kt_out[0] = k.T
    vt = v.T
    vt_out[0] = vt
    vbt_out[0, 0] = vt.astype(BF16)
    lf = _log_sigmoid(_dot(h, wf[...]) + bf[...])
    lf_out[...] = lf[:, :FOX_HEADS]
    cum = _dot3_left(tri_ref[...], lf) + carry_ref[...]
    carry_ref[...] = cum[tl - 1:tl, :]
    c3 = jnp.concatenate(_split3(cum * LOG2E), axis=1)
    qb = (_dot(c3, pq[...]) + oq[...]).astype(BF16)
    kb = (_dot(c3, pk[...]) + ok[...]).astype(BF16)
    kh = k.astype(BF16)
    for hp in range(FOX_HEADS // 2):
        a, b = hp * LANES, (hp + 1) * LANES
        q2_out[:, 2 * a:2 * a + LANES] = q[:, a:b]
        q2_out[:, 2 * a + LANES:2 * b] = qb[:, a:b]
        k2_out[:, 2 * a:2 * a + LANES] = kh[:, a:b]
        k2_out[:, 2 * a + LANES:2 * b] = kb[:, a:b]


def _fox_proj_prompt(x, g, w, tabs, *, n_batch, seq, tl):
    t, d = x.shape
    nl = seq // tl
    pq, pk, oq, ok = tabs
    tri = jnp.asarray(np.tril(np.ones((tl, tl), np.float32)), BF16)
    consts = [g, w["wq"], w["wk"], w["wv"], w["wf"], w["bf"], pq, pk, oq, ok, tri]
    row = lambda i, j: (i * nl + j, 0)
    return pl.pallas_call(
        _fox_proj_prompt_body,
        grid=(n_batch, nl),
        in_specs=[pl.BlockSpec((tl, d), row)] + [_const_spec(a.shape) for a in consts],
        out_specs=[pl.BlockSpec((1, d, tl), lambda i, j: (i, 0, j)), pl.BlockSpec((1, d, tl), lambda i, j: (i, 0, j)),
                   pl.BlockSpec((tl, FOX_HEADS), row),
                   pl.BlockSpec((tl, 2 * d), row), pl.BlockSpec((tl, 2 * d), row),
                   pl.BlockSpec((1, 1, d, tl), lambda i, j: (i, j, 0, 0))],
        out_shape=[jax.ShapeDtypeStruct((n_batch, d, seq), F32), jax.ShapeDtypeStruct((n_batch, d, seq), F32),
                   jax.ShapeDtypeStruct((t, FOX_HEADS), F32),
                   jax.ShapeDtypeStruct((t, 2 * d), BF16), jax.ShapeDtypeStruct((t, 2 * d), BF16),
                   jax.ShapeDtypeStruct((n_batch, nl, d, tl), BF16)],
        scratch_shapes=[pltpu.VMEM((1, LANES), F32)],
        compiler_params=_params(("arbitrary", "arbitrary")),
        name="fox_proj_prompt",
    )(x, *consts)


def _fold8(x, op):
    n = x.shape[0]
    while n > 8:
        n //= 2
        x = op(x[:n], x[n:])
    return x


def _fox_attn_body(q_ref, k_ref, vt_ref, o_ref, s_s, p_s, m_s, l_s, a_s, acc_s, *, tq):
    nq = q_ref.shape[0] // tq
    lane2 = lax.broadcasted_iota(jnp.int32, (1, 2 * LANES), 1) & (LANES - 1)
    sel = [jnp.where(lane2 < FOX_DH, 1.0, 0.0).astype(BF16),
           jnp.where(lane2 >= FOX_DH, 1.0, 0.0).astype(BF16)]
    row = lax.broadcasted_iota(jnp.int32, (LANES, 1), 0)
    steps = [(qi, kj) for qi in range(nq) for kj in range(qi + 1)]

    def logits(i):
        qi, kj = steps[i]
        kb = k_ref[kj * tq:(kj + 1) * tq, :]
        for hh in range(2):
            s_s[hh] = _dot_nt(kb * sel[hh], q_ref[qi * tq:(qi + 1) * tq, :])

    def softmax_pv(i):
        qi, kj = steps[i]
        diag = kj == qi
        if kj == 0:
            m_s[...] = jnp.full(m_s.shape, NEG_INF, F32)
            l_s[...] = jnp.zeros(l_s.shape, F32)
            acc_s[...] = jnp.zeros(acc_s.shape, F32)
        vt = vt_ref[0, kj]
        for hh in range(2):
            for c0 in range(0, tq, LANES):
                cs = slice(c0, c0 + LANES)
                def load_s(r0):
                    s = s_s[hh, r0:r0 + LANES, cs]
                    if diag and r0 >= c0:
                        key = lax.broadcasted_iota(jnp.int32, (LANES, LANES), 0) + r0
                        qry = lax.broadcasted_iota(jnp.int32, (LANES, LANES), 1) + c0
                        s = jnp.where(key <= qry, s, NEG_INF)
                    return s

                mx = _fold8(load_s(0), jnp.maximum)
                for r0 in range(LANES, tq, LANES):
                    mx = jnp.maximum(mx, _fold8(load_s(r0), jnp.maximum))
                m_old = m_s[hh, :, cs]
                m_new = jnp.maximum(m_old, jnp.max(mx, axis=0, keepdims=True))
                alpha = jnp.exp2(m_old - m_new)
                sm = None
                for r0 in range(0, tq, LANES):
                    p = jnp.exp2(load_s(r0) - m_new)
                    part = _fold8(p, jnp.add)
                    sm = part if sm is None else sm + part
                    p_s[hh, r0:r0 + LANES, cs] = p.astype(BF16)
                l_s[hh, :, cs] = alpha * l_s[hh, :, cs] + jnp.sum(sm, axis=0, keepdims=True)
                m_s[hh, :, cs] = m_new
                a_s[hh, :, cs] = alpha
            acc_s[hh] = a_s[hh] * acc_s[hh] + _dot(vt, p_s[hh])
        if diag:
            o_t = jnp.where(row < FOX_DH, acc_s[0] / l_s[0], acc_s[1] / l_s[1])
            o_ref[qi * tq:(qi + 1) * tq, :] = o_t.T.astype(BF16)

    for i in range(len(steps)):
        logits(i)
        softmax_pv(i)


def _fox_attn(q2, k2, vbt, *, n_batch, seq, tq):
    t = q2.shape[0]
    nq = seq // tq
    npair = FOX_HEADS // 2
    return pl.pallas_call(
        functools.partial(_fox_attn_body, tq=tq),
        grid=(n_batch, npair),
        in_specs=[pl.BlockSpec((seq, 2 * LANES), lambda b, p: (b, p)),
                  pl.BlockSpec((seq, 2 * LANES), lambda b, p: (b, p)),
                  pl.BlockSpec((1, nq, LANES, tq), lambda b, p: (b, 0, p, 0))],
        out_specs=pl.BlockSpec((seq, LANES), lambda b, p: (b, p)),
        out_shape=jax.ShapeDtypeStruct((t, D_MODEL), BF16),
        scratch_shapes=[pltpu.VMEM((2, tq, tq), F32), pltpu.VMEM((2, tq, tq), BF16),
                        pltpu.VMEM((2, 1, tq), F32), pltpu.VMEM((2, 1, tq), F32), pltpu.VMEM((2, 1, tq), F32),
                        pltpu.VMEM((2, LANES, tq), F32)],
        compiler_params=_params(("arbitrary", "arbitrary")),
        name="fox_attn",
    )(q2, k2, vbt)


def _fox_proj_sample_body(x_ref, g_ref, wq, wk, wv, wf, bf, q_out, k_out, v_out, lf_out):
    h = _rms(x_ref[...], g_ref[...]).astype(BF16)
    q_out[...] = _dot(h, wq[...]) * FOX_SCALE
    k_out[...] = _dot(h, wk[...])
    v_out[...] = _dot(h, wv[...])
    lf_out[...] = _log_sigmoid(_dot(h, wf[...]) + bf[...])[:, :FOX_HEADS]


def _fox_proj_sample(x, g, w):
    t, d = x.shape
    consts = [g, w["wq"], w["wk"], w["wv"], w["wf"], w["bf"]]
    full = lambda n: pl.BlockSpec((t, n), lambda i: (0, 0))
    return pl.pallas_call(
        _fox_proj_sample_body,
        grid=(1,),
        in_specs=[full(d)] + [_const_spec(a.shape) for a in consts],
        out_specs=[full(d), full(d), full(d), full(FOX_HEADS)],
        out_shape=[jax.ShapeDtypeStruct((t, d), F32)] * 3 + [jax.ShapeDtypeStruct((t, FOX_HEADS), F32)],
        compiler_params=_params(("arbitrary",)),
        name="fox_proj_sample",
    )(x, *consts)


def _fox_decode_body(pt_ref, q_ref, kn_ref, vn_ref, lfn_ref, *refs, n_q, pages_per_step):
    pp = pages_per_step
    page_refs = refs[:3 * pp]
    o_ref, m_s, l_s, acc_s, run_s, ccol_s, qbd_s = refs[3 * pp:]
    del pt_ref
    j = pl.program_id(1)
    hq = FOX_HEADS * n_q
    hrow = lax.broadcasted_iota(jnp.int32, (FOX_HEADS, D_MODEL), 0)
    hcol = lax.broadcasted_iota(jnp.int32, (FOX_HEADS, D_MODEL), 1) >> 6
    own = hrow == hcol

    def attend(s_parts, vt_parts):
        s = s_parts[0] if len(s_parts) == 1 else jnp.concatenate(s_parts, axis=1)
        m_old = m_s[...]
        m_new = jnp.maximum(m_old, jnp.max(s, axis=1, keepdims=True))
        alpha = jnp.exp(m_old - m_new)
        p = jnp.exp(s - m_new)
        l_s[...] = alpha * l_s[...] + jnp.sum(p, axis=1, keepdims=True)
        pb = p.astype(BF16)
        pv = _dot_nt(pb[:, 0:PAGE_SIZE], vt_parts[0])
        for i in range(1, len(vt_parts)):
            pv = pv + _dot_nt(pb[:, i * PAGE_SIZE:(i + 1) * PAGE_SIZE], vt_parts[i])
        acc_s[...] = alpha * acc_s[...] + pv
        m_s[...] = m_new

    def tile_q(bias16):
        return jnp.concatenate([bias16] * n_q, axis=0)

    @pl.when(j == 0)
    def _():
        lfn = lfn_ref[0]
        cn = lfn
        for sh in range(1, n_q):
            cn = cn + pltpu.roll(lfn, sh, axis=1)
        ccol = jnp.concatenate(
            [jnp.broadcast_to(cn[:, qq:qq + 1], (FOX_HEADS, LANES)) for qq in range(n_q)], axis=0)
        ccol_s[...] = ccol
        qbd_s[...] = jnp.concatenate(
            [jnp.where(own, jnp.broadcast_to(q_ref[0, qq:qq + 1, :], (FOX_HEADS, D_MODEL)), 0.0)
             for qq in range(n_q)], axis=0).astype(BF16)
        m_s[...] = jnp.full(m_s.shape, NEG_INF, F32)
        l_s[...] = jnp.zeros(l_s.shape, F32)
        acc_s[...] = jnp.zeros(acc_s.shape, F32)
        run_s[...] = jnp.zeros(run_s.shape, F32)
        qrow = lax.broadcasted_iota(jnp.int32, (hq, LANES), 0) >> 4
        key = lax.broadcasted_iota(jnp.int32, (hq, LANES), 1)
        s_new = _dot(qbd_s[...], kn_ref[0]) + tile_q(-cn) + ccol
        attend([jnp.where(key <= qrow, s_new, NEG_INF)], [vn_ref[0]])

    run = run_s[...]
    ccol = ccol_s[...]
    qbd = qbd_s[...]
    ji = lax.broadcasted_iota(jnp.int32, (PAGE_SIZE, PAGE_SIZE), 0)
    si = lax.broadcasted_iota(jnp.int32, (PAGE_SIZE, PAGE_SIZE), 1)
    later = jnp.where(ji > si, 1.0, 0.0).astype(BF16)
    ones = jnp.ones((PAGE_SIZE, PAGE_SIZE), BF16)
    lf_all = jnp.concatenate([page_refs[3 * i + 2][0] for i in range(pp)], axis=0)
    sfx_all = _dot3_right(lf_all, later)
    tot_all = _dot3_right(lf_all, ones)
    s_parts, vt_parts = [], []
    for i in range(pp):
        kt_ref, vt_ref, _ = page_refs[3 * i:3 * i + 3]
        hs = slice(i * FOX_HEADS, (i + 1) * FOX_HEADS)
        bias16 = sfx_all[hs, :] + run
        s_parts.append(_dot(qbd, kt_ref[0].astype(BF16)) + tile_q(bias16) + ccol)
        run = run + tot_all[hs, :]
        vt_parts.append(vt_ref[0].astype(BF16))
    run_s[...] = run
    attend(s_parts, vt_parts)

    @pl.when(j == pl.num_programs(1) - 1)
    def _():
        res = acc_s[...] / l_s[...]
        rows = [jnp.sum(jnp.where(own, res[qq * FOX_HEADS:(qq + 1) * FOX_HEADS, :], 0.0), axis=0, keepdims=True)
                for qq in range(n_q)]
        o_ref[0] = jnp.concatenate(rows, axis=0)


def _fox_decode(page_table, q, kt_new, vt_new, lf_new_t, kt_pool, vt_pool, lf_pool_t, *, n_q, pages_per_step):
    n_batch, n_pages = page_table.shape
    pp = pages_per_step
    hq = FOX_HEADS * n_q
    d = D_MODEL
    steps = n_pages // pp
    pt = page_table.reshape(-1)

    def page_map(i):
        def f(b, j, pt_ref):
            return (pt_ref[b * n_pages + (n_pages - 1 - (j * pp + i))], 0, 0)
        return f

    per_batch = lambda shape: pl.BlockSpec((1,) + shape, lambda b, j, pt_ref: (b, 0, 0))
    in_specs = [per_batch((n_q, d)), per_batch((d, PAGE_SIZE)), per_batch((d, PAGE_SIZE)),
                per_batch((FOX_HEADS, LANES))]
    args = [q, kt_new, vt_new, lf_new_t]
    for i in range(pp):
        in_specs += [pl.BlockSpec((1, d, PAGE_SIZE), page_map(i)), pl.BlockSpec((1, d, PAGE_SIZE), page_map(i)),
                     pl.BlockSpec((1, FOX_HEADS, PAGE_SIZE), page_map(i))]
        args += [kt_pool, vt_pool, lf_pool_t]
    grid_spec = pltpu.PrefetchScalarGridSpec(
        num_scalar_prefetch=1,
        grid=(n_batch, steps),
        in_specs=in_specs,
        out_specs=pl.BlockSpec((1, n_q, d), lambda b, j, pt_ref: (b, 0, 0)),
        scratch_shapes=[pltpu.VMEM((hq, 1), F32), pltpu.VMEM((hq, 1), F32), pltpu.VMEM((hq, d), F32),
                        pltpu.VMEM((FOX_HEADS, LANES), F32), pltpu.VMEM((hq, LANES), F32),
                        pltpu.VMEM((hq, d), BF16)],
    )
    return pl.pallas_call(
        functools.partial(_fox_decode_body, n_q=n_q, pages_per_step=pp),
        grid_spec=grid_spec,
        out_shape=jax.ShapeDtypeStruct((n_batch, n_q, d), F32),
        compiler_params=_params(("arbitrary", "arbitrary")),
        name="fox_decode",
    )(pt, *args)


def _conv_tail(y, x, g_post, lng, lnb, w2, b2):
    mu = jnp.mean(y, axis=-1, keepdims=True)
    yc = y - mu
    var = jnp.mean(yc * yc, axis=-1, keepdims=True)
    yn = yc * lax.rsqrt(var + LN_EPS) * lng + lnb
    m = _dot(_silu(yn).astype(BF16), w2) + b2
    return x + _rms(m, g_post)


def _glu_in(x, g_pre, w1, b1):
    h = _rms(x, g_pre).astype(BF16)
    u2 = _dot(h, w1) + b1
    return u2[:, :D_MODEL] * jax.nn.sigmoid(u2[:, D_MODEL:])


_CONV_RB = 128


def _conv_prompt_body(x_ref, g_ref, w1, b1, wdw, bdw, lng, lnb, w2, b2, o_ref, st_ref, ext_ref, y_ref):
    tl = x_ref.shape[0]
    l = pl.program_id(1)
    x = x_ref[...]

    @pl.when(l == 0)
    def _():
        ext_ref[0:CONV_HDR, :] = jnp.zeros((CONV_HDR, D_MODEL), F32)

    ext_ref[CONV_HDR:CONV_HDR + tl, :] = _glu_in(x, g_ref[0:1, :], w1[...], b1[...])
    base = CONV_HDR - (CONV_WIDTH - 1)
    sub = 8
    for r0 in range(0, tl, _CONV_RB):
        for c0 in range(0, D_MODEL, LANES):
            cs = slice(c0, c0 + LANES)
            acc = jnp.broadcast_to(bdw[:, cs], (_CONV_RB, LANES))
            for s in range(sub):
                n = _CONV_RB if s == 0 else _CONV_RB + sub
                z = None
                for a in range((base + CONV_WIDTH - 1) // sub + 1):
                    wi = sub * a + s - base
                    if 0 <= wi < CONV_WIDTH:
                        term = ext_ref[r0 + sub * a:r0 + sub * a + n, cs] * wdw[wi:wi + 1, cs]
                        z = term if z is None else z + term
                acc = acc + z[s:s + _CONV_RB]
            y_ref[r0:r0 + _CONV_RB, cs] = acc
    o_ref[...] = _conv_tail(y_ref[...], x, g_ref[1:2, :], lng[...], lnb[...], w2[...], b2[...])

    @pl.when(l == pl.num_programs(1) - 1)
    def _():
        st_ref[0] = ext_ref[CONV_HDR + tl - (CONV_WIDTH - 1):CONV_HDR + tl, :]

    ext_ref[0:CONV_HDR, :] = ext_ref[tl:tl + CONV_HDR, :]


def _conv_prompt(x, g2, w, *, n_batch, seq, tl):
    t, d = x.shape
    nl = seq // tl
    consts = [g2, w["w1"], w["b1"], w["wdw"], w["bdw"], w["lng"], w["lnb"], w["w2"], w["b2"]]
    row = lambda i, j: (i * nl + j, 0)
    return pl.pallas_call(
        _conv_prompt_body,
        grid=(n_batch, nl),
        in_specs=[pl.BlockSpec((tl, d), row)] + [_const_spec(a.shape) for a in consts],
        out_specs=[pl.BlockSpec((tl, d), row),
                   pl.BlockSpec((1, CONV_WIDTH - 1, d), lambda i, j: (i, 0, 0))],
        out_shape=[jax.ShapeDtypeStruct((t, d), F32),
                   jax.ShapeDtypeStruct((n_batch, CONV_WIDTH - 1, d), F32)],
        scratch_shapes=[pltpu.VMEM((CONV_HDR + tl, d), F32), pltpu.VMEM((tl, d), F32)],
        compiler_params=_params(("arbitrary", "arbitrary")),
        name="conv_prompt",
    )(x, *consts)


_SAMPLE_PAD = 8


def _conv_sample_body(x_ref, st_in, g_ref, w1, b1, wdw, bdw, lng, lnb, w2, b2, o_ref, st_out, ext_ref, u_ref, y_ref,
                      *, n_q):
    n_batch = st_in.shape[0]
    x = x_ref[...]
    u_ref[...] = _glu_in(x, g_ref[0:1, :], w1[...], b1[...])
    base = CONV_HDR - (CONV_WIDTH - 1)

    def per_batch(b, carry):
        r0 = pl.multiple_of(b * _SAMPLE_PAD, _SAMPLE_PAD)
        ext_ref[0:CONV_HDR, :] = st_in[b]
        ext_ref[CONV_HDR:CONV_HDR + _SAMPLE_PAD, :] = u_ref[pl.ds(r0, _SAMPLE_PAD), :]
        acc = jnp.broadcast_to(bdw[...], (_SAMPLE_PAD, D_MODEL))
        for wi in range(CONV_WIDTH):
            acc = acc + ext_ref[base + wi:base + wi + _SAMPLE_PAD, :] * wdw[wi:wi + 1, :]
        y_ref[pl.ds(r0, _SAMPLE_PAD), :] = acc
        st_out[b] = ext_ref[base + n_q:base + n_q + CONV_WIDTH - 1, :]
        return carry

    lax.fori_loop(0, n_batch, per_batch, 0)
    o_ref[...] = _conv_tail(y_ref[...], x, g_ref[1:2, :], lng[...], lnb[...], w2[...], b2[...])


def _conv_sample(x_pad, st_pad, g2, w, *, n_q):
    t, d = x_pad.shape
    n_batch = st_pad.shape[0]
    consts = [g2, w["w1"], w["b1"], w["wdw"], w["bdw"], w["lng"], w["lnb"], w["w2"], w["b2"]]
    return pl.pallas_call(
        functools.partial(_conv_sample_body, n_q=n_q),
        grid=(1,),
        in_specs=[pl.BlockSpec((t, d), lambda i: (0, 0)),
                  pl.BlockSpec(st_pad.shape, lambda i: (0, 0, 0))] + [_const_spec(a.shape) for a in consts],
        out_specs=[pl.BlockSpec((t, d), lambda i: (0, 0)),
                   pl.BlockSpec((n_batch, CONV_WIDTH - 1, d), lambda i: (0, 0, 0))],
        out_shape=[jax.ShapeDtypeStruct((t, d), F32),
                   jax.ShapeDtypeStruct((n_batch, CONV_WIDTH - 1, d), F32)],
        scratch_shapes=[pltpu.VMEM((CONV_HDR + _SAMPLE_PAD, d), F32), pltpu.VMEM((t, d), F32),
                        pltpu.VMEM((t, d), F32)],
        compiler_params=_params(("arbitrary",)),
        name="conv_sample",
    )(x_pad, st_pad, *consts)


def _row(v):
    return v.reshape(1, -1).astype(F32)


def _pad_cols(a, n):
    return jnp.pad(a, ((0, 0), (0, n - a.shape[1])))


def _gla_weights(wq, wk, wv, wg1, wg2, bg, wr, gn, wo):
    rank = wg1.shape[1]
    return dict(wq=wq.astype(BF16), wk=wk.astype(BF16), wv=wv.astype(BF16), wr=wr.astype(BF16),
                wg1=_pad_cols(wg1, LANES).astype(BF16),
                wg2=jnp.pad(wg2, ((0, LANES - rank), (0, 0))).astype(BF16),
                bg=_row(bg), gn=_row(gn), wo=wo.astype(BF16))


def _gla_sample(xs, s0, prev, g2, w, *, n_batch, n_q):
    d = xs.shape[1]
    c = GLA_CHUNK
    pr = _SAMPLE_PAD
    xpad = jnp.pad(xs.reshape(n_batch, n_q, d), ((0, 0), (0, pr - n_q), (0, 0))).reshape(n_batch * pr, d)
    out, s_fin = _gla_layer(xpad, s0, prev, g2, w, n_batch=n_batch, seq=c, nb=4, tl=c, valid=n_q, pr=pr)
    return out.reshape(n_batch, pr, d)[:, :n_q].reshape(n_batch * n_q, d), s_fin


def _fox_sample(xs, g_pre, w, k_pool, v_pool, lf_pool, page_table, *, n_batch, n_q):
    d = D_MODEL
    q, k_new, v_new, lf_new = _fox_proj_sample(xs, g_pre, w)
    new_t = lambda a: jnp.pad(a.reshape(n_batch, n_q, d).transpose(0, 2, 1),
                              ((0, 0), (0, 0), (0, PAGE_SIZE - n_q))).astype(BF16)
    lf_new_t = jnp.pad(lf_new.reshape(n_batch, n_q, FOX_HEADS).transpose(0, 2, 1),
                       ((0, 0), (0, 0), (0, LANES - n_q)))
    n_pool = lf_pool.shape[0]
    pool_t = lambda a: a.transpose(0, 2, 3, 1).reshape(n_pool, d, PAGE_SIZE)
    lf_pool_t = lf_pool.transpose(0, 2, 1)
    o = _fox_decode(page_table, q.reshape(n_batch, n_q, d), new_t(k_new), new_t(v_new), lf_new_t,
                    pool_t(k_pool), pool_t(v_pool), lf_pool_t, n_q=n_q,
                    pages_per_step=min(16, page_table.shape[1]))
    return o.reshape(n_batch * n_q, d), k_new, v_new, lf_new


def kernel(x_prompt, x_sample, state_gla, cache_fox_k, cache_fox_v, cache_fox_logf, state_conv, page_table,
           norm_g, gla_wq, gla_wk, gla_wv, gla_wg1, gla_wg2, gla_bg, gla_wr, gla_gn, gla_wo,
           fox_wq, fox_wk, fox_wv, fox_wf, fox_bf, fox_wo,
           conv_w1, conv_b1, conv_wdw, conv_bdw, conv_ln_g, conv_ln_b, conv_w2, conv_b2,
           ffn_w1, ffn_w3, ffn_w2):
    bp, seq, d = x_prompt.shape
    bs, n_q, _ = x_sample.shape
    xp = x_prompt.reshape(bp * seq, d)
    xs = x_sample.reshape(bs * n_q, d)
    tp = 512
    w1_all, w3_all, w2_all = ffn_w1.astype(BF16), ffn_w3.astype(BF16), ffn_w2.astype(BF16)
    gla_p = gla_s = None
    cv_p, cv_s = [], []
    fox_out = None
    for i in range(DEPTH):
        j = i // N_MIXERS
        kind = i % N_MIXERS
        g_mix = norm_g[i, 0:2].astype(F32)
        mix_p = mix_s = None
        if kind == 0:
            w = _gla_weights(gla_wq[j], gla_wk[j], gla_wv[j], gla_wg1[j], gla_wg2[j], gla_bg[j], gla_wr[j],
                             gla_gn[j], gla_wo[j])
            xp, gla_p = _gla_layer(xp, None, gla_p, g_mix, w, n_batch=bp, seq=seq, nb=1, tl=tp, valid=tp)
            xs, gla_s = _gla_sample(xs, (state_gla, j), gla_s, g_mix, w, n_batch=bs, n_q=n_q)
        elif kind == 1:
            w = dict(wq=fox_wq[j].astype(BF16), wk=fox_wk[j].astype(BF16), wv=fox_wv[j].astype(BF16),
                     wf=_pad_cols(fox_wf[j], LANES).astype(BF16), bf=_pad_cols(_row(fox_bf[j]), LANES))
            wo = fox_wo[j].astype(BF16)
            zero_b = jnp.zeros((1, d), F32)
            g_pre, g_post = g_mix[0:1], g_mix[1:2]
            ktp, vtp, lfp, q2, k2, vbt = _fox_proj_prompt(xp, g_pre, w, _fox_bias_tables(), n_batch=bp, seq=seq, tl=tp)
            op = _fox_attn(q2, k2, vbt, n_batch=bp, seq=seq, tq=tp)
            os_, kn, vn, lfn = _fox_sample(xs, g_pre, w, cache_fox_k[j], cache_fox_v[j], cache_fox_logf[j],
                                           page_table, n_batch=bs, n_q=n_q)
            mix_p, mix_s = (op, wo, zero_b, g_post), (os_, wo, zero_b, g_post)
            cache_view = lambda a: a.reshape(1, bp, FOX_HEADS, FOX_DH, seq).transpose(0, 1, 4, 2, 3)
            fox_out = (cache_view(ktp), cache_view(vtp),
                       lfp.reshape(1, bp, seq, FOX_HEADS),
                       kn.reshape(1, bs, n_q, FOX_HEADS, FOX_DH), vn.reshape(1, bs, n_q, FOX_HEADS, FOX_DH),
                       lfn.reshape(1, bs, n_q, FOX_HEADS))
        else:
            w = dict(w1=conv_w1[j].astype(BF16), b1=_row(conv_b1[j]), wdw=conv_wdw[j].astype(F32),
                     bdw=_row(conv_bdw[j]), lng=_row(conv_ln_g[j]), lnb=_row(conv_ln_b[j]),
                     w2=conv_w2[j].astype(BF16), b2=_row(conv_b2[j]))
            xp, stp = _conv_prompt(xp, g_mix, w, n_batch=bp, seq=seq, tl=tp)
            xs_pad = jnp.pad(xs.reshape(bs, n_q, d), ((0, 0), (0, _SAMPLE_PAD - n_q), (0, 0)))
            st_pad = jnp.pad(state_conv[j], ((0, 0), (CONV_HDR - (CONV_WIDTH - 1), 0), (0, 0)))
            xs_pad, sts = _conv_sample(xs_pad.reshape(bs * _SAMPLE_PAD, d), st_pad, g_mix, w, n_q=n_q)
            xs = xs_pad.reshape(bs, _SAMPLE_PAD, d)[:, :n_q].reshape(bs * n_q, d)
            cv_p.append(stp)
            cv_s.append(sts)
        g_ffn = norm_g[i, 2:4].astype(F32)
        xp = _ffn(xp, g_ffn, w1_all, w3_all, w2_all, i, tp, mixer=mix_p)
        xs = _ffn(xs, g_ffn, w1_all, w3_all, w2_all, i, bs * n_q, mixer=mix_s)
    return (xp.reshape(bp, seq, d), xs.reshape(bs, n_q, d), gla_p, gla_s,
            *fox_out, jnp.stack(cv_p), jnp.stack(cv_s))
```
